```python
import math
import jax, jax.numpy as jnp
from jax import lax
import numpy as np

D_MODEL = 1024
BATCH = 8
SEQ = 2048
DEPTH = 4

MIX_WIDTH = D_MODEL
GROUP_WIDTH = MIX_WIDTH // 4
ATTN_HEADS = 4
ATTN_QK_DIM = GROUP_WIDTH // ATTN_HEADS // 2
ATTN_V_DIM = 2 * ATTN_QK_DIM
Q_BLOCK = 128
CONF_KERNEL = 31
SHORT_KERNEL = 3
POOL_WINDOWS = (2, 4, 8, 16)
POOL_GROUPS = len(POOL_WINDOWS)
POOL_GROUP_DIM = GROUP_WIDTH // POOL_GROUPS
D_FF = 4 * D_MODEL
EPS = 1e-6
N_MOD = 6

A_QK = ATTN_HEADS * 2 * ATTN_QK_DIM
A_V = ATTN_HEADS * ATTN_V_DIM
B_IN = 2 * GROUP_WIDTH
C_IN = 3 * GROUP_WIDTH
D_IN = GROUP_WIDTH
IN_WIDTH = 2 * A_QK + A_V + B_IN + C_IN + D_IN
IN_SPLITS = (A_QK, 2 * A_QK, 2 * A_QK + A_V, 2 * A_QK + A_V + B_IN, 2 * A_QK + A_V + B_IN + C_IN)

kernel_name = 'hybrid_parallel_headgroup_adaln_trunk'


def rms_norm(x, g):
    xf = x.astype(jnp.float32)
    y = xf * lax.rsqrt(jnp.mean(xf * xf, axis=-1, keepdims=True) + EPS)
    return (y * g.astype(jnp.float32)).astype(x.dtype)


def layer_norm(x, g, b):
    xf = x.astype(jnp.float32)
    mu = jnp.mean(xf, axis=-1, keepdims=True)
    var = jnp.mean(jnp.square(xf - mu), axis=-1, keepdims=True)
    y = (xf - mu) * lax.rsqrt(var + EPS)
    return (y * g.astype(jnp.float32) + b.astype(jnp.float32)).astype(x.dtype)


def causal_dwconv(x, w):
    k = w.shape[0]
    ch = x.shape[-1]
    return lax.conv_general_dilated(
        x, w[:, None, :].astype(x.dtype), window_strides=(1,), padding=[(k - 1, 0)],
        dimension_numbers=('NWC', 'WIO', 'NWC'), feature_group_count=ch)


def diff_attention(q, k, v, lam_p, q_g, k_g, sub_g, lam_init):
    b, s, _ = q.shape
    q = rms_norm(q.reshape(b, s, ATTN_HEADS, 2, ATTN_QK_DIM), q_g).transpose(0, 2, 3, 1, 4)
    k = rms_norm(k.reshape(b, s, ATTN_HEADS, 2, ATTN_QK_DIM), k_g).transpose(0, 2, 3, 1, 4)
    v = v.reshape(b, s, ATTN_HEADS, ATTN_V_DIM).transpose(0, 2, 1, 3)
    lp = lam_p.astype(jnp.float32)
    lam = jnp.exp(jnp.sum(lp[0] * lp[1])) - jnp.exp(jnp.sum(lp[2] * lp[3])) + lam_init
    scale = ATTN_QK_DIM ** -0.5
    nb = s // Q_BLOCK
    qb = jnp.moveaxis(q.reshape(b, ATTN_HEADS, 2, nb, Q_BLOCK, ATTN_QK_DIM), 3, 0)
    starts = jnp.arange(nb) * Q_BLOCK
    kpos = jnp.arange(s)

    def block(args):
        qi, st = args
        sc = jnp.einsum('bhmqd,bhmkd->bhmqk', qi, k).astype(jnp.float32) * scale
        mask = kpos[None, :] <= (st + jnp.arange(Q_BLOCK))[:, None]
        p = jax.nn.softmax(jnp.where(mask, sc, -jnp.inf), axis=-1)
        a = p[:, :, 0] - lam * p[:, :, 1]
        return jnp.einsum('bhqk,bhkd->bhqd', a.astype(v.dtype), v)

    out = lax.map(block, (qb, starts))
    out = jnp.moveaxis(out, 0, 2).reshape(b, ATTN_HEADS, s, ATTN_V_DIM)
    out = rms_norm(out, sub_g) * (1.0 - lam_init)
    return out.transpose(0, 2, 1, 3).reshape(b, s, ATTN_HEADS * ATTN_V_DIM)


def conformer_conv(u, dw_w, dw_b, ln_g, ln_b):
    val, gate = jnp.split(u, 2, axis=-1)
    h = val * jax.nn.sigmoid(gate)
    h = causal_dwconv(h, dw_w) + dw_b
    return jax.nn.silu(layer_norm(h, ln_g, ln_b))


def short_gated_conv(u, w):
    bg, cg, xc = jnp.split(u, 3, axis=-1)
    return bg * causal_dwconv(cg * xc, w)


def multiscale_pool(u, w_pool, scale):
    b, s, _ = u.shape
    xf = u.reshape(b, s, POOL_GROUPS, POOL_GROUP_DIM).astype(jnp.float32)
    cs = jnp.concatenate([jnp.zeros((b, 1, POOL_GROUPS, POOL_GROUP_DIM), jnp.float32),
                          jnp.cumsum(xf, axis=1)], axis=1)
    t = np.arange(s)
    pooled = []
    for g, w in enumerate(POOL_WINDOWS):
        lo = np.maximum(t + 1 - w, 0)
        cnt = jnp.asarray(np.minimum(t + 1, w).astype(np.float32))
        pooled.append((cs[:, 1:, g] - cs[:, lo, g]) / cnt[None, :, None])
    y = (jnp.stack(pooled, axis=2) - xf).astype(u.dtype)
    y = jnp.einsum('bsgc,gcd->bsgd', y, w_pool)
    return y.reshape(b, s, GROUP_WIDTH) * scale


def setup_inputs(seed: int = 0) -> dict:
    key = jax.random.key(seed)
    ks = jax.random.split(key, 24)
    f32 = jnp.float32
    nrm = lambda k, shape, sd: jax.random.normal(k, shape, f32) * sd
    L = DEPTH
    return {
        'x': nrm(ks[0], (BATCH, SEQ, D_MODEL), 1.0),
        'c': nrm(ks[1], (BATCH, D_MODEL), 1.0),
        'w_ada': nrm(ks[2], (L, D_MODEL, N_MOD * D_MODEL), D_MODEL ** -0.5),
        'b_ada': nrm(ks[3], (L, N_MOD * D_MODEL), 0.02),
        'norm1_g': 1.0 + nrm(ks[4], (L, D_MODEL), 0.02),
        'norm2_g': 1.0 + nrm(ks[5], (L, D_MODEL), 0.02),
        'w_in': nrm(ks[6], (L, D_MODEL, IN_WIDTH), D_MODEL ** -0.5),
        'w_out': nrm(ks[7], (L, MIX_WIDTH, D_MODEL), MIX_WIDTH ** -0.5),
        'q_norm_g': 1.0 + nrm(ks[8], (L, ATTN_QK_DIM), 0.02),
        'k_norm_g': 1.0 + nrm(ks[9], (L, ATTN_QK_DIM), 0.02),
        'lam_params': nrm(ks[10], (L, 4, ATTN_QK_DIM), 0.1),
        'attn_sub_g': 1.0 + nrm(ks[11], (L, ATTN_V_DIM), 0.02),
        'conf_dw_w': nrm(ks[12], (L, CONF_KERNEL, GROUP_WIDTH), CONF_KERNEL ** -0.5),
        'conf_dw_b': nrm(ks[13], (L, GROUP_WIDTH), 0.02),
        'conf_ln_g': 1.0 + nrm(ks[14], (L, GROUP_WIDTH), 0.02),
        'conf_ln_b': nrm(ks[15], (L, GROUP_WIDTH), 0.02),
        'short_conv_w': nrm(ks[16], (L, SHORT_KERNEL, GROUP_WIDTH), SHORT_KERNEL ** -0.5),
        'pool_w': nrm(ks[17], (L, POOL_GROUPS, POOL_GROUP_DIM, POOL_GROUP_DIM), POOL_GROUP_DIM ** -0.5),
        'pool_scale': 1.0 + nrm(ks[18], (L, GROUP_WIDTH), 0.1),
        'w_ff1': nrm(ks[19], (L, D_MODEL, D_FF), D_MODEL ** -0.5),
        'w_ff2': nrm(ks[20], (L, D_FF, D_MODEL), D_FF ** -0.5),
    }


def reference(x, c, w_ada, b_ada, norm1_g, norm2_g, w_in, w_out, q_norm_g, k_norm_g, lam_params,
              attn_sub_g, conf_dw_w, conf_dw_b, conf_ln_g, conf_ln_b, short_conv_w, pool_w, pool_scale,
              w_ff1, w_ff2):
    c_act = jax.nn.silu(c)
    for l in range(DEPTH):
        lam_init = 0.8 - 0.6 * math.exp(-0.3 * l)
        mod = (c_act @ w_ada[l] + b_ada[l])[:, None, :]
        sh1, sc1, g1, sh2, sc2, g2 = jnp.split(mod, N_MOD, axis=-1)
        h = rms_norm(x, norm1_g[l]) * (1.0 + sc1) + sh1
        proj = h @ w_in[l]
        qa, ka, va, ub, uc, ud = jnp.split(proj, IN_SPLITS, axis=-1)
        ya = diff_attention(qa, ka, va, lam_params[l], q_norm_g[l], k_norm_g[l], attn_sub_g[l], lam_init)
        yb = conformer_conv(ub, conf_dw_w[l], conf_dw_b[l], conf_ln_g[l], conf_ln_b[l])
        yc = short_gated_conv(uc, short_conv_w[l])
        yd = multiscale_pool(ud, pool_w[l], pool_scale[l])
        mix = jnp.concatenate([ya, yb, yc, yd], axis=-1) @ w_out[l]
        x = x + g1 * mix
        h = rms_norm(x, norm2_g[l]) * (1.0 + sc2) + sh2
        x = x + g2 * (jnp.square(jax.nn.relu(h @ w_ff1[l])) @ w_ff2[l])
    return x
```

```python
import functools
import math

import jax
import jax.numpy as jnp
from jax import lax
from jax.experimental import pallas as pl
from jax.experimental.pallas import tpu as pltpu

F32 = jnp.float32
BF16 = jnp.bfloat16

EPS = 1e-6
N_MOD = 6
ATTN_HEADS = 4
QK_DIM = 32
V_DIM = 64
CONF_KERNEL = 31
SHORT_KERNEL = 3
POOL_WINDOWS = (2, 4, 8, 16)
GROUP_WIDTH = 256
LANES = 128

ROW_TILE = 512
ATTN_TILE = 512
CONV_CHUNK = 128
CONF_HALO = 32
SHORT_HALO = 8
POOL_HALO = 16
VMEM_LIMIT = 56 * 1024 * 1024


def _split_bf16(x):
    hi = x.astype(BF16)
    lo = (x - hi.astype(F32)).astype(BF16)
    return hi, lo


def _group_sum(x, ones_blockdiag):
    hi, lo = _split_bf16(x)
    return (jnp.dot(hi, ones_blockdiag, preferred_element_type=F32)
            + jnp.dot(lo, ones_blockdiag, preferred_element_type=F32))


def _ada_kernel(c_ref, w_ref, b_ref, o_ref):
    c = c_ref[...]
    c_act = c * jax.nn.sigmoid(c)
    c_hi, c_lo = _split_bf16(c_act)
    w = w_ref[0]
    w_hi, w_lo = _split_bf16(w)
    acc = jnp.dot(c_hi, w_hi, preferred_element_type=F32)
    acc += jnp.dot(c_lo, w_hi, preferred_element_type=F32)
    acc += jnp.dot(c_hi, w_lo, preferred_element_type=F32)
    o_ref[0] = acc + b_ref[0]


def _ada_call(c, w_ada, b_ada):
    depth, d, n = w_ada.shape
    b = c.shape[0]
    tn = n // 4
    return pl.pallas_call(
        _ada_kernel,
        grid=(depth, n // tn),
        in_specs=[
            pl.BlockSpec((b, d), lambda l, j: (0, 0)),
            pl.BlockSpec((1, d, tn), lambda l, j: (l, 0, j)),
            pl.BlockSpec((1, 1, tn), lambda l, j: (l, 0, j)),
        ],
        out_specs=pl.BlockSpec((1, b, tn), lambda l, j: (l, 0, j)),
        out_shape=jax.ShapeDtypeStruct((depth, b, n), F32),
        compiler_params=pltpu.CompilerParams(
            dimension_semantics=("arbitrary", "arbitrary"), vmem_limit_bytes=VMEM_LIMIT),
        name="ada_mod",
    )(c, w_ada, b_ada.reshape(depth, 1, n))


def _modulated_norm(x, gain, shift, scale):
    ms = jnp.mean(x * x, axis=-1, keepdims=True)
    return (x * lax.rsqrt(ms + EPS) * gain) * (1.0 + scale) + shift


def _inproj_kernel(x_ref, mod_ref, g_ref, w_ref, qkg_ref, ones32_ref, o_ref):
    h = _modulated_norm(x_ref[0], g_ref[...], mod_ref[0, 0:1, :], mod_ref[0, 1:2, :]).astype(BF16)
    n_out = o_ref.shape[-1]
    qk_width = 2 * GROUP_WIDTH
    for n in range(0, n_out, GROUP_WIDTH):
        p = jnp.dot(h, w_ref[:, n:n + GROUP_WIDTH], preferred_element_type=F32)
        if n < qk_width:
            ms = _group_sum(p * p, ones32_ref[...]) * (1.0 / QK_DIM)
            p = p * lax.rsqrt(ms + EPS) * qkg_ref[:, n:n + GROUP_WIDTH]
        o_ref[0, :, n:n + GROUP_WIDTH] = p.astype(BF16)


def _inproj_call(x, mod, norm_g, w_in, qk_gain, ones32):
    b, s, d = x.shape
    n = w_in.shape[1]
    tm = ROW_TILE
    return pl.pallas_call(
        _inproj_kernel,
        grid=(b, s // tm),
        in_specs=[
            pl.BlockSpec((1, tm, d), lambda i, j: (i, j, 0)),
            pl.BlockSpec((1, N_MOD, d), lambda i, j: (i, 0, 0)),
            pl.BlockSpec((1, d), lambda i, j: (0, 0)),
            pl.BlockSpec((d, n), lambda i, j: (0, 0)),
            pl.BlockSpec((1, 2 * GROUP_WIDTH), lambda i, j: (0, 0)),
            pl.BlockSpec((GROUP_WIDTH, GROUP_WIDTH), lambda i, j: (0, 0)),
        ],
        out_specs=pl.BlockSpec((1, tm, n), lambda i, j: (i, j, 0)),
        out_shape=jax.ShapeDtypeStruct((b, s, n), BF16),
        compiler_params=pltpu.CompilerParams(
            dimension_semantics=("arbitrary", "arbitrary"), vmem_limit_bytes=VMEM_LIMIT),
        name="in_proj",
    )(x, mod, norm_g, w_in, qk_gain, ones32)


def _attn_kernel(q_ref, k_ref, v_ref, lamp_ref, subg_ref, ones64_ref, o_ref,
                 qm_scr, m_scr, l_scr, acc_scr, *, lam_init):
    t = q_ref.shape[1]
    width = q_ref.shape[2]
    n_maps = 2 * ATTN_HEADS
    qi = pl.program_id(1)
    lane = lax.broadcasted_iota(jnp.int32, (1, width), 1)
    head_of_lane = lane // V_DIM
    map_of_lane = lane // QK_DIM

    q = q_ref[0]
    for g in range(n_maps):
        qm_scr[g] = jnp.where(map_of_lane == g, q, jnp.zeros_like(q))
    m_scr[...] = jnp.full(m_scr.shape, -jnp.inf, F32)
    l_scr[...] = jnp.zeros(l_scr.shape, F32)
    acc_scr[...] = jnp.zeros(acc_scr.shape, F32)

    def lanes_per_head(vals):
        out = jnp.broadcast_to(vals[0], (t, width))
        for h in range(1, ATTN_HEADS):
            out = jnp.where(head_of_lane == h, vals[h], out)
        return out

    def step(j, diagonal):
        start = pl.multiple_of(j * t, t)
        kb = k_ref[0, pl.ds(start, t), :]
        vb = v_ref[0, pl.ds(start, t), :]
        vm = [jnp.where(head_of_lane == h, vb, jnp.zeros_like(vb)) for h in range(ATTN_HEADS)]
        if diagonal:
            row = lax.broadcasted_iota(jnp.int32, (t, t), 0)
            col = lax.broadcasted_iota(jnp.int32, (t, t), 1)
            keep = col <= row
        for mp in range(2):
            pv = None
            alphas = []
            for h in range(ATTN_HEADS):
                g = 2 * h + mp
                s = lax.dot_general(qm_scr[g], kb, (((1,), (1,)), ((), ())),
                                    preferred_element_type=F32)
                if diagonal:
                    s = jnp.where(keep, s, -jnp.inf)
                m_prev = m_scr[g]
                m_new = jnp.maximum(m_prev, jnp.max(s, axis=1, keepdims=True))
                alpha = jnp.exp(m_prev - m_new)
                p = jnp.exp(s - jnp.tile(m_new, (1, t // LANES)))
                p_sum = p[:, 0:LANES]
                for c in range(1, t // LANES):
                    p_sum = p_sum + p[:, c * LANES:(c + 1) * LANES]
                l_scr[g] = alpha * l_scr[g] + p_sum
                m_scr[g] = m_new
                alphas.append(alpha[:, 0:1])
                d = jnp.dot(p.astype(BF16), vm[h], preferred_element_type=F32)
                pv = d if pv is None else pv + d
            acc_scr[mp] = acc_scr[mp] * lanes_per_head(alphas) + pv

    def body(j, carry):
        step(j, False)
        return carry

    lax.fori_loop(0, qi, body, 0)
    step(qi, True)

    lp = lamp_ref[...]
    lam = (jnp.exp(jnp.sum(lp[0:1] * lp[1:2], axis=-1, keepdims=True))
           - jnp.exp(jnp.sum(lp[2:3] * lp[3:4], axis=-1, keepdims=True)) + lam_init)
    inv = []
    for mp in range(2):
        cols = [1.0 / jnp.sum(l_scr[2 * h + mp], axis=1, keepdims=True) for h in range(ATTN_HEADS)]
        inv.append(lanes_per_head(cols))
    out = acc_scr[0] * inv[0] - lam * (acc_scr[1] * inv[1])
    ms = _group_sum(out * out, ones64_ref[...]) * (1.0 / V_DIM)
    out = out * lax.rsqrt(ms + EPS) * subg_ref[...] * (1.0 - lam_init)
    o_ref[0] = out.astype(o_ref.dtype)


def _attn_call(proj, lam_params, sub_gain, ones64, lam_init):
    b, s, _ = proj.shape
    t = ATTN_TILE
    w = GROUP_WIDTH
    n_maps = 2 * ATTN_HEADS
    return pl.pallas_call(
        functools.partial(_attn_kernel, lam_init=lam_init),
        grid=(b, s // t),
        in_specs=[
            pl.BlockSpec((1, t, w), lambda i, j: (i, j, 0)),
            pl.BlockSpec((1, s, w), lambda i, j: (i, 0, 1)),
            pl.BlockSpec((1, s, w), lambda i, j: (i, 0, 2)),
            pl.BlockSpec((4, QK_DIM), lambda i, j: (0, 0)),
            pl.BlockSpec((1, w), lambda i, j: (0, 0)),
            pl.BlockSpec((w, w), lambda i, j: (0, 0)),
        ],
        out_specs=pl.BlockSpec((1, t, w), lambda i, j: (i, j, 0)),
        out_shape=jax.ShapeDtypeStruct((b, s, w), BF16),
        scratch_shapes=[
            pltpu.VMEM((n_maps, t, w), BF16),
            pltpu.VMEM((n_maps, t, LANES), F32),
            pltpu.VMEM((n_maps, t, LANES), F32),
            pltpu.VMEM((2, t, w), F32),
        ],
        compiler_params=pltpu.CompilerParams(
            dimension_semantics=("arbitrary", "arbitrary"), vmem_limit_bytes=VMEM_LIMIT),
        name="diff_attn",
    )(proj, proj, proj, lam_params, sub_gain, ones64)


def _mixers_kernel(bval_ref, bgate_ref, cb_ref, cc_ref, cx_ref, dx_ref,
                   cw_ref, cbias_ref, lng_ref, lnb_ref, sw_ref, pw_ref, ps_ref,
                   o_ref, h_scr, c_scr, d_scr):
    ts = bval_ref.shape[1]
    w = GROUP_WIDTH
    j = pl.program_id(1)

    @pl.when(j == 0)
    def _():
        h_scr[0:CONF_HALO, :] = jnp.zeros((CONF_HALO, w), F32)
        c_scr[0:SHORT_HALO, :] = jnp.zeros((SHORT_HALO, w), F32)
        d_scr[0:POOL_HALO, :] = jnp.zeros((POOL_HALO, w), F32)

    val = bval_ref[0].astype(F32)
    gate = bgate_ref[0].astype(F32)
    h_scr[CONF_HALO:CONF_HALO + ts, :] = val * jax.nn.sigmoid(gate)
    for r in range(0, ts, CONV_CHUNK):
        acc = jnp.broadcast_to(cbias_ref[...], (CONV_CHUNK, w))
        for k in range(CONF_KERNEL):
            off = CONF_HALO - (CONF_KERNEL - 1) + k + r
            acc = acc + cw_ref[k:k + 1, :] * h_scr[off:off + CONV_CHUNK, :]
        mu = jnp.mean(acc, axis=-1, keepdims=True)
        cen = acc - mu
        var = jnp.mean(cen * cen, axis=-1, keepdims=True)
        y = cen * lax.rsqrt(var + EPS) * lng_ref[...] + lnb_ref[...]
        o_ref[0, r:r + CONV_CHUNK, 0:w] = (y * jax.nn.sigmoid(y)).astype(o_ref.dtype)
    h_scr[0:CONF_HALO, :] = h_scr[ts:ts + CONF_HALO, :]

    c_scr[SHORT_HALO:SHORT_HALO + ts, :] = cc_ref[0].astype(F32) * cx_ref[0].astype(F32)
    for r in range(0, ts, CONV_CHUNK):
        acc = None
        for k in range(SHORT_KERNEL):
            off = SHORT_HALO - (SHORT_KERNEL - 1) + k + r
            term = sw_ref[k:k + 1, :] * c_scr[off:off + CONV_CHUNK, :]
            acc = term if acc is None else acc + term
        o_ref[0, r:r + CONV_CHUNK, w:2 * w] = (
            cb_ref[0, r:r + CONV_CHUNK, :].astype(F32) * acc).astype(o_ref.dtype)
    c_scr[0:SHORT_HALO, :] = c_scr[ts:ts + SHORT_HALO, :]

    d_scr[POOL_HALO:POOL_HALO + ts, :] = dx_ref[0].astype(F32)
    lane = lax.broadcasted_iota(jnp.int32, (1, w), 1)
    group = lane // (w // len(POOL_WINDOWS))
    win = jnp.zeros((1, w), jnp.int32)
    for g, width in enumerate(POOL_WINDOWS):
        win = jnp.where(group == g, width, win)
    for r in range(0, ts, CONV_CHUNK):
        x_tok = d_scr[POOL_HALO + r:POOL_HALO + r + CONV_CHUNK, :]
        run = x_tok
        pooled = None
        for i in range(1, max(POOL_WINDOWS)):
            run = run + d_scr[POOL_HALO + r - i:POOL_HALO + r - i + CONV_CHUNK, :]
            if i + 1 in POOL_WINDOWS:
                g = POOL_WINDOWS.index(i + 1)
                pooled = run if pooled is None else jnp.where(group >= g, run, pooled)
        t_pos = j * ts + r + lax.broadcasted_iota(jnp.int32, (CONV_CHUNK, w), 0)
        cnt = jnp.minimum(t_pos + 1, win).astype(F32)
        y = (pooled / cnt - x_tok).astype(BF16)
        yd = jnp.dot(y, pw_ref[...], preferred_element_type=F32) * ps_ref[...]
        o_ref[0, r:r + CONV_CHUNK, 2 * w:3 * w] = yd.astype(o_ref.dtype)
    d_scr[0:POOL_HALO, :] = d_scr[ts:ts + POOL_HALO, :]


def _mixers_call(proj, conf_w, conf_b, ln_g, ln_b, short_w, pool_bd, pool_scale):
    b, s, _ = proj.shape
    ts = ROW_TILE
    w = GROUP_WIDTH
    col = lambda c: pl.BlockSpec((1, ts, w), lambda i, j: (i, j, c))
    const = lambda shape: pl.BlockSpec(shape, lambda i, j: (0, 0))
    return pl.pallas_call(
        _mixers_kernel,
        grid=(b, s // ts),
        in_specs=[col(3), col(4), col(5), col(6), col(7), col(8),
                  const((CONF_KERNEL, w)), const((1, w)), const((1, w)), const((1, w)),
                  const((SHORT_KERNEL, w)), const((w, w)), const((1, w))],
        out_specs=pl.BlockSpec((1, ts, 3 * w), lambda i, j: (i, j, 0)),
        out_shape=jax.ShapeDtypeStruct((b, s, 3 * w), BF16),
        scratch_shapes=[
            pltpu.VMEM((CONF_HALO + ts, w), F32),
            pltpu.VMEM((SHORT_HALO + ts, w), F32),
            pltpu.VMEM((POOL_HALO + ts, w), F32),
        ],
        compiler_params=pltpu.CompilerParams(
            dimension_semantics=("arbitrary", "arbitrary"), vmem_limit_bytes=VMEM_LIMIT),
        name="mixers",
    )(proj, proj, proj, proj, proj, proj,
      conf_w, conf_b, ln_g, ln_b, short_w, pool_bd, pool_scale)


def _outproj_mlp_kernel(x_ref, ya_ref, yo_ref, mod_ref, g_ref, wout_ref, w1_ref, w2_ref,
                        o_ref, x1_scr, a_scr):
    w = GROUP_WIDTH
    d_ff = w1_ref.shape[1]
    mix = jnp.dot(ya_ref[0], wout_ref[0:w, :], preferred_element_type=F32)
    mix = mix + jnp.dot(yo_ref[0], wout_ref[w:, :], preferred_element_type=F32)
    x1 = x_ref[0] + mod_ref[0, 2:3, :] * mix
    x1_scr[...] = x1
    h = _modulated_norm(x1, g_ref[...], mod_ref[0, 3:4, :], mod_ref[0, 4:5, :]).astype(BF16)
    chunk = 1024
    for c in range(0, d_ff, chunk):
        a = jnp.dot(h, w1_ref[:, c:c + chunk], preferred_element_type=F32)
        a_scr[:, c:c + chunk] = jnp.square(jnp.maximum(a, 0.0)).astype(BF16)
    y = jnp.dot(a_scr[...], w2_ref[...], preferred_element_type=F32)
    o_ref[0] = x1_scr[...] + mod_ref[0, 5:6, :] * y


def _outproj_mlp_call(x, ya, yo, mod, norm_g, w_out, w1, w2):
    b, s, d = x.shape
    d_ff = w1.shape[1]
    tm = ROW_TILE
    resident = lambda shape: pl.BlockSpec(shape, lambda i, j: (0, 0), pipeline_mode=pl.Buffered(1))
    return pl.pallas_call(
        _outproj_mlp_kernel,
        grid=(b, s // tm),
        in_specs=[
            pl.BlockSpec((1, tm, d), lambda i, j: (i, j, 0)),
            pl.BlockSpec((1, tm, ya.shape[2]), lambda i, j: (i, j, 0)),
            pl.BlockSpec((1, tm, yo.shape[2]), lambda i, j: (i, j, 0)),
            pl.BlockSpec((1, N_MOD, d), lambda i, j: (i, 0, 0)),
            pl.BlockSpec((1, d), lambda i, j: (0, 0)),
            resident(w_out.shape), resident(w1.shape), resident(w2.shape),
        ],
        out_specs=pl.BlockSpec((1, tm, d), lambda i, j: (i, j, 0)),
        out_shape=jax.ShapeDtypeStruct((b, s, d), F32),
        scratch_shapes=[pltpu.VMEM((tm, d), F32), pltpu.VMEM((tm, d_ff), BF16)],
        compiler_params=pltpu.CompilerParams(
            dimension_semantics=("arbitrary", "arbitrary"), vmem_limit_bytes=VMEM_LIMIT),
        name="outproj_mlp",
    )(x, ya, yo, mod, norm_g, w_out, w1, w2)


def _block_diag_ones(width, group):
    idx = jnp.arange(width) // group
    return (idx[:, None] == idx[None, :]).astype(BF16)


def kernel(x, c, w_ada, b_ada, norm1_g, norm2_g, w_in, w_out, q_norm_g, k_norm_g, lam_params,
           attn_sub_g, conf_dw_w, conf_dw_b, conf_ln_g, conf_ln_b, short_conv_w, pool_w, pool_scale,
           w_ff1, w_ff2):
    depth, d, _ = w_in.shape
    b = x.shape[0]
    w = GROUP_WIDTH
    mods = _ada_call(c, w_ada, b_ada).reshape(depth, b, N_MOD, d)
    ones32 = _block_diag_ones(w, QK_DIM)
    ones64 = _block_diag_ones(w, V_DIM)
    n_pool = len(POOL_WINDOWS)
    for l in range(depth):
        lam_init = 0.8 - 0.6 * math.exp(-0.3 * l)
        qk_gain = jnp.concatenate([jnp.tile(q_norm_g[l], 2 * ATTN_HEADS) * (QK_DIM ** -0.5),
                                   jnp.tile(k_norm_g[l], 2 * ATTN_HEADS)]).reshape(1, 2 * w)
        sub_gain = jnp.tile(attn_sub_g[l], ATTN_HEADS).reshape(1, w)
        pool_bd = jax.scipy.linalg.block_diag(*[pool_w[l, g] for g in range(n_pool)]).astype(BF16)
        row = lambda v: v.reshape(1, -1)

        proj = _inproj_call(x, mods[l], row(norm1_g[l]), w_in[l].astype(BF16), qk_gain, ones32)
        ya = _attn_call(proj, lam_params[l], sub_gain, ones64, lam_init)
        yo = _mixers_call(proj, conf_dw_w[l], row(conf_dw_b[l]), row(conf_ln_g[l]), row(conf_ln_b[l]),
                          short_conv_w[l], pool_bd, row(pool_scale[l]))
        x = _outproj_mlp_call(x, ya, yo, mods[l], row(norm2_g[l]), w_out[l].astype(BF16),
                              w_ff1[l].astype(BF16), w_ff2[l].astype(BF16))
    return x
```

```python
import functools
import math

import jax
import jax.numpy as jnp
from jax import lax
from jax.experimental import pallas as pl
from jax.experimental.pallas import tpu as pltpu

F32 = jnp.float32
BF16 = jnp.bfloat16

EPS = 1e-6
LOG2_E = math.log2(math.e)
N_MOD = 6
ATTN_HEADS = 4
QK_DIM = 32
V_DIM = 64
VT_ROWS = V_DIM + 16
N_MAPS = 2 * ATTN_HEADS
CONF_KERNEL = 31
SHORT_KERNEL = 3
POOL_WINDOWS = (2, 4, 8, 16)
GROUP_WIDTH = 256
SUBLANES = 8

ROW_TILE = 512
ATTN_TILE = 256
CONV_CHUNK = 64
HALO = 32
VMEM_LIMIT = 56 * 1024 * 1024


def _split_bf16(x):
    hi = x.astype(BF16)
    lo = (x - hi.astype(F32)).astype(BF16)
    return hi, lo


def _group_sum(x, ones_blockdiag):
    hi, lo = _split_bf16(x)
    return (jnp.dot(hi, ones_blockdiag, preferred_element_type=F32)
            + jnp.dot(lo, ones_blockdiag, preferred_element_type=F32))


def _modulated_norm(x, gain, shift, scale):
    ms = jnp.mean(x * x, axis=-1, keepdims=True)
    return (x * lax.rsqrt(ms + EPS) * gain) * (1.0 + scale) + shift


def _ada_kernel(c_ref, w_ref, b_ref, o_ref):
    c = c_ref[...]
    c_act = c * jax.nn.sigmoid(c)
    c_hi, c_lo = _split_bf16(c_act)
    w_hi, w_lo = _split_bf16(w_ref[0])
    acc = jnp.dot(c_hi, w_hi, preferred_element_type=F32)
    acc += jnp.dot(c_lo, w_hi, preferred_element_type=F32)
    acc += jnp.dot(c_hi, w_lo, preferred_element_type=F32)
    o_ref[0] = acc + b_ref[0]


def _ada_call(c, w_ada, b_ada):
    depth, d, n = w_ada.shape
    b = c.shape[0]
    tn = n // 4
    return pl.pallas_call(
        _ada_kernel,
        grid=(depth, n // tn),
        in_specs=[
            pl.BlockSpec((b, d), lambda l, j: (0, 0)),
            pl.BlockSpec((1, d, tn), lambda l, j: (l, 0, j)),
            pl.BlockSpec((1, 1, tn), lambda l, j: (l, 0, j)),
        ],
        out_specs=pl.BlockSpec((1, b, tn), lambda l, j: (l, 0, j)),
        out_shape=jax.ShapeDtypeStruct((depth, b, n), F32),
        compiler_params=pltpu.CompilerParams(
            dimension_semantics=("arbitrary", "arbitrary"), vmem_limit_bytes=VMEM_LIMIT),
        name="ada_mod",
    )(c, w_ada, b_ada.reshape(depth, 1, n))


def _conformer_mixer(h_scr, ph_scr, cw_ref, cbias_ref, lng_ref, lnb_ref, out_ref, ts):
    w = GROUP_WIDTH
    first = HALO - (CONF_KERNEL - 1)
    phases = []
    for phase in range(SUBLANES):
        taps = [k for k in range(CONF_KERNEL) if (first + k) % SUBLANES == phase]
        base = first + taps[0]
        if phase == 0:
            phases.append((taps, h_scr, base))
        else:
            span = taps[-1] - taps[0] + ts
            ph_scr[phase, 0:span, :] = h_scr[base:base + span, :]
            phases.append((taps, ph_scr.at[phase], 0))
    for r in range(0, ts, CONV_CHUNK):
        acc = jnp.broadcast_to(cbias_ref[...], (CONV_CHUNK, w))
        for taps, src, base in phases:
            for k in taps:
                o = r + base + k - taps[0]
                acc = acc + cw_ref[k:k + 1, :] * src[o:o + CONV_CHUNK, :]
        mu = jnp.mean(acc, axis=-1, keepdims=True)
        cen = acc - mu
        var = jnp.mean(cen * cen, axis=-1, keepdims=True)
        y = cen * lax.rsqrt(var + EPS) * lng_ref[...] + lnb_ref[...]
        out_ref[0, r:r + CONV_CHUNK, 0:w] = (y * jax.nn.sigmoid(y)).astype(out_ref.dtype)


def _inproj_mixers_kernel(x_ref, mod_ref, g_ref, w_ref, qkg_ref, ones32_ref,
                          cw_ref, cbias_ref, lng_ref, lnb_ref, sw_ref, pw_ref, ps_ref,
                          qkv_ref, yo_ref, h_scr, c_scr, d_scr, e_scr, f_scr, ph_scr):
    ts = x_ref.shape[1]
    w = GROUP_WIDTH
    j = pl.program_id(1)

    @pl.when(j == 0)
    def _():
        zeros = jnp.zeros((HALO, w), F32)
        h_scr[0:HALO, :] = zeros
        c_scr[0:HALO, :] = zeros
        d_scr[0:HALO, :] = zeros

    h = _modulated_norm(x_ref[0], g_ref[0], mod_ref[0, 0, 0:1, :], mod_ref[0, 0, 1:2, :]).astype(BF16)

    def proj(col):
        return jnp.dot(h, w_ref[0, :, col * w:(col + 1) * w], preferred_element_type=F32)

    for col in range(2):
        p = proj(col)
        ms = _group_sum(p * p, ones32_ref[...]) * (1.0 / QK_DIM)
        p = p * lax.rsqrt(ms + EPS) * qkg_ref[0, :, col * w:(col + 1) * w]
        qkv_ref[0, :, col * w:(col + 1) * w] = p.astype(qkv_ref.dtype)
    qkv_ref[0, :, 2 * w:3 * w] = proj(2).astype(qkv_ref.dtype)

    h_scr[HALO:HALO + ts, :] = proj(3) * jax.nn.sigmoid(proj(4))
    _conformer_mixer(h_scr, ph_scr, cw_ref.at[0], cbias_ref.at[0], lng_ref.at[0], lnb_ref.at[0],
                     yo_ref, ts)
    h_scr[0:HALO, :] = h_scr[ts:ts + HALO, :]

    gate = proj(5)
    c_scr[HALO:HALO + ts, :] = proj(6) * proj(7)
    conv = None
    for k in range(SHORT_KERNEL):
        off = HALO - (SHORT_KERNEL - 1) + k
        term = sw_ref[0, k:k + 1, :] * c_scr[off:off + ts, :]
        conv = term if conv is None else conv + term
    yo_ref[0, :, w:2 * w] = (gate * conv).astype(yo_ref.dtype)
    c_scr[0:HALO, :] = c_scr[ts:ts + HALO, :]

    d_scr[HALO:HALO + ts, :] = proj(8)
    lane = lax.broadcasted_iota(jnp.int32, (1, w), 1)
    group = lane // (w // len(POOL_WINDOWS))
    e_scr[8:, :] = d_scr[8:, :] + d_scr[7:HALO + ts - 1, :]
    pooled = e_scr[HALO:, :]
    win = jnp.full((1, w), POOL_WINDOWS[0], jnp.int32)
    src, dst = e_scr, f_scr
    for g in range(1, len(POOL_WINDOWS)):
        shift = POOL_WINDOWS[g - 1]
        lo = SUBLANES * (g + 1)
        dst[lo:, :] = src[lo:, :] + src[lo - shift:HALO + ts - shift, :]
        pooled = jnp.where(group >= g, dst[HALO:, :], pooled)
        win = jnp.where(group >= g, POOL_WINDOWS[g], win)
        src, dst = dst, src
    t_pos = j * ts + lax.broadcasted_iota(jnp.int32, (ts, w), 0)
    cnt = jnp.minimum(t_pos + 1, win).astype(F32)
    y = (pooled / cnt - d_scr[HALO:, :]).astype(BF16)
    yd = jnp.dot(y, pw_ref[0], preferred_element_type=F32) * ps_ref[0]
    yo_ref[0, :, 2 * w:3 * w] = yd.astype(yo_ref.dtype)
    d_scr[0:HALO, :] = d_scr[ts:ts + HALO, :]


def _inproj_mixers_call(layer, x, mods, norm_g, w_in, qk_gain, ones32,
                        conf_w, conf_b, ln_g, ln_b, short_w, pool_bd, pool_scale):
    b, s, d = x.shape
    n = w_in.shape[2]
    ts = ROW_TILE
    w = GROUP_WIDTH
    per_layer = lambda *shape: pl.BlockSpec((1,) + shape, lambda i, j: (layer,) + (0,) * len(shape))
    return pl.pallas_call(
        _inproj_mixers_kernel,
        grid=(b, s // ts),
        in_specs=[
            pl.BlockSpec((1, ts, d), lambda i, j: (i, j, 0)),
            pl.BlockSpec((1, 1, N_MOD, d), lambda i, j: (layer, i, 0, 0)),
            per_layer(1, d),
            per_layer(d, n),
            per_layer(1, 2 * w),
            pl.BlockSpec((w, w), lambda i, j: (0, 0)),
            per_layer(CONF_KERNEL, w), per_layer(1, w), per_layer(1, w), per_layer(1, w),
            per_layer(SHORT_KERNEL, w), per_layer(w, w), per_layer(1, w),
        ],
        out_specs=[pl.BlockSpec((1, ts, 3 * w), lambda i, j: (i, j, 0)),
                   pl.BlockSpec((1, ts, 3 * w), lambda i, j: (i, j, 0))],
        out_shape=[jax.ShapeDtypeStruct((b, s, 3 * w), BF16),
                   jax.ShapeDtypeStruct((b, s, 3 * w), BF16)],
        scratch_shapes=[pltpu.VMEM((HALO + ts, w), F32) for _ in range(5)]
        + [pltpu.VMEM((SUBLANES, HALO + ts, w), F32)],
        compiler_params=pltpu.CompilerParams(
            dimension_semantics=("arbitrary", "arbitrary"), vmem_limit_bytes=VMEM_LIMIT),
        name="inproj_mixers",
    )(x, mods, norm_g, w_in, qk_gain, ones32, conf_w, conf_b, ln_g, ln_b, short_w, pool_bd, pool_scale)


def _attn_kernel(q_ref, k_ref, v_ref, lamp_ref, subg_ref, o_ref,
                 vt_scr, qmt_scr, sa_scr, sb_scr, m_scr, acc_scr, ot_scr, *, lam_init):
    t = q_ref.shape[1]
    n_chunks = k_ref.shape[1] // t
    qi = pl.program_id(1)

    @pl.when(qi == 0)
    def _():
        for c in range(n_chunks):
            vt = v_ref[0, c * t:(c + 1) * t, :].astype(F32).T.astype(BF16)
            for head in range(ATTN_HEADS):
                r = head * VT_ROWS
                vt_scr[c, r:r + V_DIM, :] = vt[head * V_DIM:(head + 1) * V_DIM, :]
                vt_scr[c, r + V_DIM:r + VT_ROWS, :] = jnp.ones((VT_ROWS - V_DIM, t), BF16)

    qt = q_ref[0].astype(F32).T
    map_of_row = lax.broadcasted_iota(jnp.int32, qt.shape, 0) // QK_DIM
    for g in range(N_MAPS):
        qmt_scr[g] = jnp.where(map_of_row == g, qt, 0.0).astype(BF16)
    m_scr[...] = jnp.full(m_scr.shape, -jnp.inf, F32)
    acc_scr[...] = jnp.zeros(acc_scr.shape, F32)

    def fold(x, op):
        parts = [x[r:r + SUBLANES, :] for r in range(0, x.shape[0], SUBLANES)]
        while len(parts) > 1:
            parts = [op(parts[i], parts[i + 1]) for i in range(0, len(parts), 2)]
        return parts[0]

    def stage(score_chunk, score_dst, soft_chunk, soft_src, diagonal=False):
        if score_chunk is not None:
            kb = k_ref[0, pl.ds(pl.multiple_of(score_chunk * t, t), t), :]
        if diagonal:
            key = lax.broadcasted_iota(jnp.int32, (t, t), 0)
            query = lax.broadcasted_iota(jnp.int32, (t, t), 1)
            keep = key <= query
        for g in range(N_MAPS):
            if score_chunk is not None:
                score_dst[g] = jnp.dot(kb, qmt_scr[g], preferred_element_type=F32)
            if soft_chunk is None:
                continue
            head = g // 2
            st = soft_src[g]
            if diagonal:
                st = jnp.where(keep, st, -jnp.inf)
            m_prev = m_scr[g]
            m_new = jnp.maximum(m_prev, jnp.max(fold(st, jnp.maximum), axis=0, keepdims=True))
            alpha = jnp.exp2(m_prev - m_new)
            p = jnp.exp2((st - m_new).astype(BF16))
            m_scr[g] = m_new
            vt = vt_scr[soft_chunk, head * VT_ROWS:(head + 1) * VT_ROWS, :]
            pv = jnp.dot(vt, p, preferred_element_type=F32)
            acc_scr[g] = acc_scr[g] * alpha + pv

    stage(0, sa_scr, None, None)

    def body(i, carry):
        stage(2 * i + 1, sb_scr, 2 * i, sa_scr)
        stage(2 * i + 2, sa_scr, 2 * i + 1, sb_scr)
        return carry

    lax.fori_loop(0, qi // 2, body, 0)

    @pl.when(qi % 2 == 0)
    def _():
        stage(None, None, qi, sa_scr, diagonal=True)

    @pl.when(qi % 2 == 1)
    def _():
        stage(qi, sb_scr, qi - 1, sa_scr)
        stage(None, None, qi, sb_scr, diagonal=True)

    lp = lamp_ref[0]
    lam = (jnp.exp(jnp.sum(lp[0:1] * lp[1:2], axis=-1, keepdims=True))
           - jnp.exp(jnp.sum(lp[2:3] * lp[3:4], axis=-1, keepdims=True)) + lam_init)
    for head in range(ATTN_HEADS):
        o = []
        for mp in range(2):
            g = 2 * head + mp
            o.append(acc_scr[g, 0:V_DIM, :] / acc_scr[g, V_DIM:V_DIM + 1, :])
        out = o[0] - lam * o[1]
        ms = jnp.mean(out * out, axis=0, keepdims=True)
        out = out * lax.rsqrt(ms + EPS) * subg_ref[0] * (1.0 - lam_init)
        ot_scr[head * V_DIM:(head + 1) * V_DIM, :] = out
    o_ref[0] = ot_scr[...].T.astype(o_ref.dtype)


def _attn_call(layer, qkv, lam_params, sub_gain, lam_init):
    b, s, _ = qkv.shape
    t = ATTN_TILE
    w = GROUP_WIDTH
    return pl.pallas_call(
        functools.partial(_attn_kernel, lam_init=lam_init),
        grid=(b, s // t),
        in_specs=[
            pl.BlockSpec((1, t, w), lambda i, j: (i, j, 0)),
            pl.BlockSpec((1, s, w), lambda i, j: (i, 0, 1)),
            pl.BlockSpec((1, s, w), lambda i, j: (i, 0, 2)),
            pl.BlockSpec((1, 4, QK_DIM), lambda i, j: (layer, 0, 0)),
            pl.BlockSpec((1, V_DIM, t), lambda i, j: (layer, 0, 0)),
        ],
        out_specs=pl.BlockSpec((1, t, w), lambda i, j: (i, j, 0)),
        out_shape=jax.ShapeDtypeStruct((b, s, w), BF16),
        scratch_shapes=[
            pltpu.VMEM((s // t, ATTN_HEADS * VT_ROWS, t), BF16),
            pltpu.VMEM((N_MAPS, w, t), BF16),
            pltpu.VMEM((N_MAPS, t, t), F32),
            pltpu.VMEM((N_MAPS, t, t), F32),
            pltpu.VMEM((N_MAPS, 1, t), F32),
            pltpu.VMEM((N_MAPS, VT_ROWS, t), F32),
            pltpu.VMEM((w, t), F32),
        ],
        compiler_params=pltpu.CompilerParams(
            dimension_semantics=("arbitrary", "arbitrary"), vmem_limit_bytes=VMEM_LIMIT),
        name="diff_attn",
    )(qkv, qkv, qkv, lam_params, sub_gain)


def _outproj_mlp_kernel(x_ref, ya_ref, yo_ref, mod_ref, g_ref, wout_ref, w1_ref, w2_ref,
                        o_ref, x1_scr, a_scr):
    w = GROUP_WIDTH
    d_ff = w1_ref.shape[2]
    mod = lambda r: mod_ref[0, 0, r:r + 1, :]
    mix = jnp.dot(ya_ref[0], wout_ref[0, 0:w, :], preferred_element_type=F32)
    mix = mix + jnp.dot(yo_ref[0], wout_ref[0, w:, :], preferred_element_type=F32)
    x1 = x_ref[0] + mod(2) * mix
    x1_scr[...] = x1
    h = _modulated_norm(x1, g_ref[0], mod(3), mod(4)).astype(BF16)
    chunk = 1024
    for c in range(0, d_ff, chunk):
        a = jnp.dot(h, w1_ref[0, :, c:c + chunk], preferred_element_type=F32)
        a_scr[:, c:c + chunk] = jnp.square(jnp.maximum(a, 0.0)).astype(BF16)
    y = jnp.dot(a_scr[...], w2_ref[0], preferred_element_type=F32)
    o_ref[0] = x1_scr[...] + mod(5) * y


def _outproj_mlp_call(layer, x, ya, yo, mods, norm_g, w_out, w1, w2):
    b, s, d = x.shape
    d_ff = w1.shape[2]
    tm = ROW_TILE
    resident = lambda arr: pl.BlockSpec((1,) + arr.shape[1:], lambda i, j: (layer, 0, 0),
                                        pipeline_mode=pl.Buffered(1))
    return pl.pallas_call(
        _outproj_mlp_kernel,
        grid=(b, s // tm),
        in_specs=[
            pl.BlockSpec((1, tm, d), lambda i, j: (i, j, 0)),
            pl.BlockSpec((1, tm, ya.shape[2]), lambda i, j: (i, j, 0)),
            pl.BlockSpec((1, tm, yo.shape[2]), lambda i, j: (i, j, 0)),
            pl.BlockSpec((1, 1, N_MOD, d), lambda i, j: (layer, i, 0, 0)),
            pl.BlockSpec((1, 1, d), lambda i, j: (layer, 0, 0)),
            resident(w_out), resident(w1), resident(w2),
        ],
        out_specs=pl.BlockSpec((1, tm, d), lambda i, j: (i, j, 0)),
        out_shape=jax.ShapeDtypeStruct((b, s, d), F32),
        scratch_shapes=[pltpu.VMEM((tm, d), F32), pltpu.VMEM((tm, d_ff), BF16)],
        compiler_params=pltpu.CompilerParams(
            dimension_semantics=("arbitrary", "arbitrary"), vmem_limit_bytes=VMEM_LIMIT),
        name="outproj_mlp",
    )(x, ya, yo, mods, norm_g, w_out, w1, w2)


def _block_diag_ones(width, group):
    idx = jnp.arange(width) // group
    return (idx[:, None] == idx[None, :]).astype(BF16)


def kernel(x, c, w_ada, b_ada, norm1_g, norm2_g, w_in, w_out, q_norm_g, k_norm_g, lam_params,
           attn_sub_g, conf_dw_w, conf_dw_b, conf_ln_g, conf_ln_b, short_conv_w, pool_w, pool_scale,
           w_ff1, w_ff2):
    depth, d, _ = w_in.shape
    b = x.shape[0]
    w = GROUP_WIDTH
    mods = _ada_call(c, w_ada, b_ada).reshape(depth, b, N_MOD, d)
    ones32 = _block_diag_ones(w, QK_DIM)
    rows = lambda v: v.reshape(depth, 1, -1)

    qk_gain = jnp.concatenate([jnp.tile(q_norm_g, (1, N_MAPS)) * (QK_DIM ** -0.5 * LOG2_E),
                               jnp.tile(k_norm_g, (1, N_MAPS))], axis=1).reshape(depth, 1, 2 * w)
    sub_gain = jnp.broadcast_to(attn_sub_g[:, :, None], (depth, V_DIM, ATTN_TILE))
    n_pool = len(POOL_WINDOWS)
    group_eye = jnp.eye(n_pool, dtype=pool_w.dtype)
    pool_bd = jnp.einsum('lgcd,gh->lgchd', pool_w, group_eye).reshape(depth, w, w).astype(BF16)
    w_in_b, w_out_b = w_in.astype(BF16), w_out.astype(BF16)
    w_ff1_b, w_ff2_b = w_ff1.astype(BF16), w_ff2.astype(BF16)

    for l in range(depth):
        lam_init = 0.8 - 0.6 * math.exp(-0.3 * l)
        qkv, yo = _inproj_mixers_call(
            l, x, mods, rows(norm1_g), w_in_b, qk_gain, ones32,
            conf_dw_w, rows(conf_dw_b), rows(conf_ln_g), rows(conf_ln_b),
            short_conv_w, pool_bd, rows(pool_scale))
        ya = _attn_call(l, qkv, lam_params, sub_gain, lam_init)
        x = _outproj_mlp_call(l, x, ya, yo, mods, rows(norm2_g), w_out_b, w_ff1_b, w_ff2_b)
    return x
```

```python
import functools
import math

import jax
import jax.numpy as jnp
from jax import lax
from jax.experimental import pallas as pl
from jax.experimental.pallas import tpu as pltpu

F32 = jnp.float32
BF16 = jnp.bfloat16

EPS = 1e-6
LOG2_E = math.log2(math.e)
N_MOD = 6
ATTN_HEADS = 4
QK_DIM = 32
V_DIM = 64
VT_ROWS = V_DIM + 16
N_MAPS = 2 * ATTN_HEADS
CONF_KERNEL = 31
SHORT_KERNEL = 3
POOL_WINDOWS = (2, 4, 8, 16)
GROUP_WIDTH = 256
SUBLANES = 8

ROW_TILE = 512
INPROJ_TILE = 1024
SUB_TILE = 256
ATTN_TILE = 256
CONV_CHUNK = 64
HALO = 32
VMEM_LIMIT = 56 * 1024 * 1024


def _split_bf16(x):
    hi = x.astype(BF16)
    lo = (x - hi.astype(F32)).astype(BF16)
    return hi, lo


def _group_sum(x, ones_blockdiag):
    return jnp.dot(x.astype(BF16), ones_blockdiag, preferred_element_type=F32)


def _modulated_norm(x, gain, shift, scale):
    ms = jnp.mean(x * x, axis=-1, keepdims=True)
    return (x * lax.rsqrt(ms + EPS)) * (gain * (1.0 + scale)) + shift


def _ada_kernel(c_ref, w_ref, b_ref, o_ref):
    c = c_ref[...]
    c_act = c * jax.nn.sigmoid(c)
    c_hi, c_lo = _split_bf16(c_act)
    w_hi, w_lo = _split_bf16(w_ref[0])
    acc = jnp.dot(c_hi, w_hi, preferred_element_type=F32)
    acc += jnp.dot(c_lo, w_hi, preferred_element_type=F32)
    acc += jnp.dot(c_hi, w_lo, preferred_element_type=F32)
    o_ref[0] = acc + b_ref[0]


def _ada_call(c, w_ada, b_ada):
    depth, d, n = w_ada.shape
    b = c.shape[0]
    tn = n // 4
    return pl.pallas_call(
        _ada_kernel,
        grid=(depth, n // tn),
        in_specs=[
            pl.BlockSpec((b, d), lambda l, j: (0, 0)),
            pl.BlockSpec((1, d, tn), lambda l, j: (l, 0, j)),
            pl.BlockSpec((1, 1, tn), lambda l, j: (l, 0, j)),
        ],
        out_specs=pl.BlockSpec((1, b, tn), lambda l, j: (l, 0, j)),
        out_shape=jax.ShapeDtypeStruct((depth, b, n), F32),
        compiler_params=pltpu.CompilerParams(
            dimension_semantics=("arbitrary", "arbitrary"), vmem_limit_bytes=VMEM_LIMIT),
        name="ada_mod",
    )(c, w_ada, b_ada.reshape(depth, 1, n))


def _conformer_pieces(h_scr, ph_scr, cw_ref, cbias_ref, lng_ref, lnb_ref, out_ref, r0, rows):
    w = GROUP_WIDTH
    first = HALO - (CONF_KERNEL - 1)
    phases = []
    for phase in range(SUBLANES):
        taps = [k for k in range(CONF_KERNEL) if (first + k) % SUBLANES == phase]
        src = (h_scr, first + taps[0]) if phase == 0 else (ph_scr.at[phase], 0)
        phases.append((taps,) + src)

    def shifted_copies():
        for phase in range(1, SUBLANES):
            taps = phases[phase][0]
            base = first + taps[0]
            span = taps[-1] - taps[0] + rows
            ph_scr[phase, r0:r0 + span, :] = h_scr[r0 + base:r0 + base + span, :]

    def chunk(r):
        acc = jnp.broadcast_to(cbias_ref[...], (CONV_CHUNK, w))
        for taps, src, base in phases:
            for k in taps:
                o = r + base + k - taps[0]
                acc = acc + cw_ref[k:k + 1, :] * src[o:o + CONV_CHUNK, :]
        mu = jnp.mean(acc, axis=-1, keepdims=True)
        cen = acc - mu
        var = jnp.mean(cen * cen, axis=-1, keepdims=True)
        y = cen * lax.rsqrt(var + EPS) * lng_ref[...] + lnb_ref[...]
        out_ref[0, r:r + CONV_CHUNK, 0:w] = (y * jax.nn.sigmoid(y)).astype(out_ref.dtype)

    return [shifted_copies] + [functools.partial(chunk, r) for r in range(r0, r0 + rows, CONV_CHUNK)]


def _inproj_mixers_kernel(x_ref, mod_ref, g_ref, w_ref, qkg_ref, ones32_ref,
                          cw_ref, cbias_ref, lng_ref, lnb_ref, sw_ref, pw_ref, ps_ref,
                          qkv_ref, yo_ref,
                          hb_scr, pa_scr, pb_scr, h_scr, c_scr, d_scr, e_scr, f_scr, ph_scr):
    ts = x_ref.shape[1]
    w = GROUP_WIDTH
    n_cols = w_ref.shape[2] // w
    rows = SUB_TILE
    j = pl.program_id(1)

    @pl.when(j == 0)
    def _():
        zeros = jnp.zeros((HALO, w), F32)
        h_scr[0:HALO, :] = zeros
        c_scr[0:HALO, :] = zeros
        d_scr[0:HALO, :] = zeros

    lane = lax.broadcasted_iota(jnp.int32, (1, w), 1)
    group = lane // (w // len(POOL_WINDOWS))
    col = lambda c: slice(c * w, (c + 1) * w)

    def matmul_pieces(r0, dst):
        def normalise():
            hb_scr[...] = _modulated_norm(x_ref[0, r0:r0 + rows, :], g_ref[0], mod_ref[0, 0, 0:1, :],
                                          mod_ref[0, 0, 1:2, :]).astype(BF16)

        def project(c):
            dst[:, col(c)] = jnp.dot(hb_scr[...], w_ref[0, :, col(c)], preferred_element_type=F32)

        return [normalise] + [functools.partial(project, c) for c in range(n_cols)]

    def vector_pieces(r0, src):
        tile = slice(r0, r0 + rows)
        scr = slice(HALO + r0, HALO + r0 + rows)

        def qk_norm(c):
            p = src[:, col(c)]
            ms = _group_sum(p * p, ones32_ref[...]) * (1.0 / QK_DIM)
            p = p * lax.rsqrt(ms + EPS) * qkg_ref[0, :, col(c)]
            qkv_ref[0, tile, col(c)] = p.astype(qkv_ref.dtype)

        def v_and_glu():
            qkv_ref[0, tile, col(2)] = src[:, col(2)].astype(qkv_ref.dtype)
            h_scr[scr, :] = src[:, col(3)] * jax.nn.sigmoid(src[:, col(4)])

        def short_conv():
            c_scr[scr, :] = src[:, col(6)] * src[:, col(7)]
            conv = None
            for k in range(SHORT_KERNEL):
                off = HALO + r0 - (SHORT_KERNEL - 1) + k
                term = sw_ref[0, k:k + 1, :] * c_scr[off:off + rows, :]
                conv = term if conv is None else conv + term
            yo_ref[0, tile, col(1)] = (src[:, col(5)] * conv).astype(yo_ref.dtype)

        def pool():
            d_scr[scr, :] = src[:, col(8)]
            end = HALO + r0 + rows
            lo = r0 + SUBLANES
            e_scr[lo:end, :] = d_scr[lo:end, :] + d_scr[lo - 1:end - 1, :]
            pooled = e_scr[scr, :]
            win = jnp.full((1, w), POOL_WINDOWS[0], jnp.int32)
            a, b = e_scr, f_scr
            for g in range(1, len(POOL_WINDOWS)):
                shift = POOL_WINDOWS[g - 1]
                lo = r0 + SUBLANES * (g + 1)
                b[lo:end, :] = a[lo:end, :] + a[lo - shift:end - shift, :]
                pooled = jnp.where(group >= g, b[scr, :], pooled)
                win = jnp.where(group >= g, POOL_WINDOWS[g], win)
                a, b = b, a
            t_pos = j * ts + r0 + lax.broadcasted_iota(jnp.int32, (rows, w), 0)
            cnt = jnp.minimum(t_pos + 1, win).astype(F32)
            y = (pooled / cnt - d_scr[scr, :]).astype(BF16)
            yd = jnp.dot(y, pw_ref[0], preferred_element_type=F32) * ps_ref[0]
            yo_ref[0, tile, col(2)] = yd.astype(yo_ref.dtype)

        conformer = _conformer_pieces(h_scr, ph_scr, cw_ref.at[0], cbias_ref.at[0], lng_ref.at[0],
                                      lnb_ref.at[0], yo_ref, r0, rows)
        return ([functools.partial(qk_norm, 0), functools.partial(qk_norm, 1), v_and_glu]
                + conformer + [short_conv, pool])

    buffers = (pa_scr, pb_scr)
    n_sub = ts // rows
    for i in range(n_sub + 1):
        mm = matmul_pieces(i * rows, buffers[i % 2]) if i < n_sub else []
        vec = vector_pieces((i - 1) * rows, buffers[(i - 1) % 2]) if i > 0 else []
        for k in range(max(len(mm), len(vec))):
            if k < len(mm):
                mm[k]()
            if k < len(vec):
                vec[k]()

    for buf in (h_scr, c_scr, d_scr):
        buf[0:HALO, :] = buf[ts:ts + HALO, :]


def _inproj_mixers_call(layer, x, mods, norm_g, w_in, qk_gain, ones32,
                        conf_w, conf_b, ln_g, ln_b, short_w, pool_bd, pool_scale):
    b, s, d = x.shape
    n = w_in.shape[2]
    ts = INPROJ_TILE
    w = GROUP_WIDTH
    per_layer = lambda *shape: pl.BlockSpec((1,) + shape, lambda i, j: (layer,) + (0,) * len(shape))
    return pl.pallas_call(
        _inproj_mixers_kernel,
        grid=(b, s // ts),
        in_specs=[
            pl.BlockSpec((1, ts, d), lambda i, j: (i, j, 0)),
            pl.BlockSpec((1, 1, N_MOD, d), lambda i, j: (layer, i, 0, 0)),
            per_layer(1, d),
            per_layer(d, n),
            per_layer(1, 2 * w),
            pl.BlockSpec((w, w), lambda i, j: (0, 0)),
            per_layer(CONF_KERNEL, w), per_layer(1, w), per_layer(1, w), per_layer(1, w),
            per_layer(SHORT_KERNEL, w), per_layer(w, w), per_layer(1, w),
        ],
        out_specs=[pl.BlockSpec((1, ts, 3 * w), lambda i, j: (i, j, 0)),
                   pl.BlockSpec((1, ts, 3 * w), lambda i, j: (i, j, 0))],
        out_shape=[jax.ShapeDtypeStruct((b, s, 3 * w), BF16),
                   jax.ShapeDtypeStruct((b, s, 3 * w), BF16)],
        scratch_shapes=[pltpu.VMEM((SUB_TILE, d), BF16),
                        pltpu.VMEM((SUB_TILE, n), F32), pltpu.VMEM((SUB_TILE, n), F32)]
        + [pltpu.VMEM((HALO + ts, w), F32) for _ in range(5)]
        + [pltpu.VMEM((SUBLANES, HALO + ts, w), F32)],
        compiler_params=pltpu.CompilerParams(
            dimension_semantics=("arbitrary", "arbitrary"), vmem_limit_bytes=VMEM_LIMIT),
        name="inproj_mixers",
    )(x, mods, norm_g, w_in, qk_gain, ones32, conf_w, conf_b, ln_g, ln_b, short_w, pool_bd, pool_scale)


def _attn_kernel(q_ref, k_ref, v_ref, lamp_ref, subg_ref, o_ref,
                 vt_scr, qmt_scr, sa_scr, sb_scr, m_scr, acc_scr, ot_scr, *, lam_init):
    t = q_ref.shape[1]
    n_chunks = k_ref.shape[1] // t
    qi = pl.program_id(1)

    @pl.when(qi == 0)
    def _():
        for c in range(n_chunks):
            vt = v_ref[0, c * t:(c + 1) * t, :].astype(F32).T.astype(BF16)
            for head in range(ATTN_HEADS):
                r = head * VT_ROWS
                vt_scr[c, r:r + V_DIM, :] = vt[head * V_DIM:(head + 1) * V_DIM, :]
                vt_scr[c, r + V_DIM:r + VT_ROWS, :] = jnp.ones((VT_ROWS - V_DIM, t), BF16)

    qt = q_ref[0].astype(F32).T
    map_of_row = lax.broadcasted_iota(jnp.int32, qt.shape, 0) // QK_DIM
    for g in range(N_MAPS):
        qmt_scr[g] = jnp.where(map_of_row == g, qt, 0.0).astype(BF16)
    m_scr[...] = jnp.full(m_scr.shape, -jnp.inf, F32)
    acc_scr[...] = jnp.zeros(acc_scr.shape, F32)

    def fold(x, op):
        parts = [x[r:r + SUBLANES, :] for r in range(0, x.shape[0], SUBLANES)]
        while len(parts) > 1:
            parts = [op(parts[i], parts[i + 1]) for i in range(0, len(parts), 2)]
        return parts[0]

    def stage(score_chunk, score_dst, soft_chunk, soft_src, diagonal=False):
        if score_chunk is not None:
            kb = k_ref[0, pl.ds(pl.multiple_of(score_chunk * t, t), t), :]
        if diagonal:
            key = lax.broadcasted_iota(jnp.int32, (t, t), 0)
            query = lax.broadcasted_iota(jnp.int32, (t, t), 1)
            keep = key <= query
        for g in range(N_MAPS):
            if score_chunk is not None:
                score_dst[g] = jnp.dot(kb, qmt_scr[g], preferred_element_type=F32)
            if soft_chunk is None:
                continue
            head = g // 2
            st = soft_src[g]
            if diagonal:
                st = jnp.where(keep, st, -jnp.inf)
            m_prev = m_scr[g]
            m_new = jnp.maximum(m_prev, jnp.max(fold(st, jnp.maximum), axis=0, keepdims=True))
            alpha = jnp.exp2(m_prev - m_new)
            p = jnp.exp2((st - m_new).astype(BF16))
            m_scr[g] = m_new
            vt = vt_scr[soft_chunk, head * VT_ROWS:(head + 1) * VT_ROWS, :]
            pv = jnp.dot(vt, p, preferred_element_type=F32)
            acc_scr[g] = acc_scr[g] * alpha + pv

    stage(0, sa_scr, None, None)

    def body(i, carry):
        stage(2 * i + 1, sb_scr, 2 * i, sa_scr)
        stage(2 * i + 2, sa_scr, 2 * i + 1, sb_scr)
        return carry

    lax.fori_loop(0, qi // 2, body, 0)

    @pl.when(qi % 2 == 0)
    def _():
        stage(None, None, qi, sa_scr, diagonal=True)

    @pl.when(qi % 2 == 1)
    def _():
        stage(qi, sb_scr, qi - 1, sa_scr)
        stage(None, None, qi, sb_scr, diagonal=True)

    lp = lamp_ref[0]
    lam = (jnp.exp(jnp.sum(lp[0:1] * lp[1:2], axis=-1, keepdims=True))
           - jnp.exp(jnp.sum(lp[2:3] * lp[3:4], axis=-1, keepdims=True)) + lam_init)
    for head in range(ATTN_HEADS):
        o = []
        for mp in range(2):
            g = 2 * head + mp
            o.append(acc_scr[g, 0:V_DIM, :] / acc_scr[g, V_DIM:V_DIM + 1, :])
        out = o[0] - lam * o[1]
        ms = jnp.mean(out * out, axis=0, keepdims=True)
        out = out * lax.rsqrt(ms + EPS) * subg_ref[0] * (1.0 - lam_init)
        ot_scr[head * V_DIM:(head + 1) * V_DIM, :] = out
    o_ref[0] = ot_scr[...].T.astype(o_ref.dtype)


def _attn_call(layer, qkv, lam_params, sub_gain, lam_init):
    b, s, _ = qkv.shape
    t = ATTN_TILE
    w = GROUP_WIDTH
    return pl.pallas_call(
        functools.partial(_attn_kernel, lam_init=lam_init),
        grid=(b, s // t),
        in_specs=[
            pl.BlockSpec((1, t, w), lambda i, j: (i, j, 0)),
            pl.BlockSpec((1, s, w), lambda i, j: (i, 0, 1)),
            pl.BlockSpec((1, s, w), lambda i, j: (i, 0, 2)),
            pl.BlockSpec((1, 4, QK_DIM), lambda i, j: (layer, 0, 0)),
            pl.BlockSpec((1, V_DIM, t), lambda i, j: (layer, 0, 0)),
        ],
        out_specs=pl.BlockSpec((1, t, w), lambda i, j: (i, j, 0)),
        out_shape=jax.ShapeDtypeStruct((b, s, w), BF16),
        scratch_shapes=[
            pltpu.VMEM((s // t, ATTN_HEADS * VT_ROWS, t), BF16),
            pltpu.VMEM((N_MAPS, w, t), BF16),
            pltpu.VMEM((N_MAPS, t, t), F32),
            pltpu.VMEM((N_MAPS, t, t), F32),
            pltpu.VMEM((N_MAPS, 1, t), F32),
            pltpu.VMEM((N_MAPS, VT_ROWS, t), F32),
            pltpu.VMEM((w, t), F32),
        ],
        compiler_params=pltpu.CompilerParams(
            dimension_semantics=("arbitrary", "arbitrary"), vmem_limit_bytes=VMEM_LIMIT),
        name="diff_attn",
    )(qkv, qkv, qkv, lam_params, sub_gain)


def _outproj_mlp_kernel(x_ref, ya_ref, yo_ref, mod_ref, g_ref, wout_ref, w1_ref, w2_ref,
                        o_ref, x1_scr, a_scr):
    w = GROUP_WIDTH
    d_ff = w1_ref.shape[2]
    mod = lambda r: mod_ref[0, 0, r:r + 1, :]
    mix = jnp.dot(ya_ref[0], wout_ref[0, 0:w, :], preferred_element_type=F32)
    mix = mix + jnp.dot(yo_ref[0], wout_ref[0, w:, :], preferred_element_type=F32)
    x1 = x_ref[0] + mod(2) * mix
    x1_scr[...] = x1
    h = _modulated_norm(x1, g_ref[0], mod(3), mod(4)).astype(BF16)
    chunk = 1024
    for c in range(0, d_ff, chunk):
        a = jnp.dot(h, w1_ref[0, :, c:c + chunk], preferred_element_type=F32)
        a_scr[:, c:c + chunk] = jnp.square(jnp.maximum(a, 0.0)).astype(BF16)
    y = jnp.dot(a_scr[...], w2_ref[0], preferred_element_type=F32)
    o_ref[0] = x1_scr[...] + mod(5) * y


def _outproj_mlp_call(layer, x, ya, yo, mods, norm_g, w_out, w1, w2):
    b, s, d = x.shape
    d_ff = w1.shape[2]
    tm = ROW_TILE
    resident = lambda arr: pl.BlockSpec((1,) + arr.shape[1:], lambda i, j: (layer, 0, 0),
                                        pipeline_mode=pl.Buffered(1))
    return pl.pallas_call(
        _outproj_mlp_kernel,
        grid=(b, s // tm),
        in_specs=[
            pl.BlockSpec((1, tm, d), lambda i, j: (i, j, 0)),
            pl.BlockSpec((1, tm, ya.shape[2]), lambda i, j: (i, j, 0)),
            pl.BlockSpec((1, tm, yo.shape[2]), lambda i, j: (i, j, 0)),
            pl.BlockSpec((1, 1, N_MOD, d), lambda i, j: (layer, i, 0, 0)),
            pl.BlockSpec((1, 1, d), lambda i, j: (layer, 0, 0)),
            resident(w_out), resident(w1), resident(w2),
        ],
        out_specs=pl.BlockSpec((1, tm, d), lambda i, j: (i, j, 0)),
        out_shape=jax.ShapeDtypeStruct((b, s, d), F32),
        scratch_shapes=[pltpu.VMEM((tm, d), F32), pltpu.VMEM((tm, d_ff), BF16)],
        compiler_params=pltpu.CompilerParams(
            dimension_semantics=("arbitrary", "arbitrary"), vmem_limit_bytes=VMEM_LIMIT),
        name="outproj_mlp",
    )(x, ya, yo, mods, norm_g, w_out, w1, w2)


def _block_diag_ones(width, group):
    idx = jnp.arange(width) // group
    return (idx[:, None] == idx[None, :]).astype(BF16)


def kernel(x, c, w_ada, b_ada, norm1_g, norm2_g, w_in, w_out, q_norm_g, k_norm_g, lam_params,
           attn_sub_g, conf_dw_w, conf_dw_b, conf_ln_g, conf_ln_b, short_conv_w, pool_w, pool_scale,
           w_ff1, w_ff2):
    depth, d, _ = w_in.shape
    b = x.shape[0]
    w = GROUP_WIDTH
    mods = _ada_call(c, w_ada, b_ada).reshape(depth, b, N_MOD, d)
    ones32 = _block_diag_ones(w, QK_DIM)
    rows = lambda v: v.reshape(depth, 1, -1)

    qk_gain = jnp.concatenate([jnp.tile(q_norm_g, (1, N_MAPS)) * (QK_DIM ** -0.5 * LOG2_E),
                               jnp.tile(k_norm_g, (1, N_MAPS))], axis=1).reshape(depth, 1, 2 * w)
    sub_gain = jnp.broadcast_to(attn_sub_g[:, :, None], (depth, V_DIM, ATTN_TILE))
    n_pool = len(POOL_WINDOWS)
    group_eye = jnp.eye(n_pool, dtype=pool_w.dtype)
    pool_bd = jnp.einsum('lgcd,gh->lgchd', pool_w, group_eye).reshape(depth, w, w).astype(BF16)
    w_in_b, w_out_b = w_in.astype(BF16), w_out.astype(BF16)
    w_ff1_b, w_ff2_b = w_ff1.astype(BF16), w_ff2.astype(BF16)

    for l in range(depth):
        lam_init = 0.8 - 0.6 * math.exp(-0.3 * l)
        qkv, yo = _inproj_mixers_call(
            l, x, mods, rows(norm1_g), w_in_b, qk_gain, ones32,
            conf_dw_w, rows(conf_dw_b), rows(conf_ln_g), rows(conf_ln_b),
            short_conv_w, pool_bd, rows(pool_scale))
        ya = _attn_call(l, qkv, lam_params, sub_gain, lam_init)
        x = _outproj_mlp_call(l, x, ya, yo, mods, rows(norm2_g), w_out_b, w_ff1_b, w_ff2_b)
    return x
```

```python
import functools
import math

import jax
import jax.numpy as jnp
from jax import lax
from jax.experimental import pallas as pl
from jax.experimental.pallas import tpu as pltpu

F32 = jnp.float32
BF16 = jnp.bfloat16

EPS = 1e-6
LOG2_E = math.log2(math.e)
N_MOD = 6
ATTN_HEADS = 4
QK_DIM = 32
V_DIM = 64
VT_ROWS = V_DIM + 16
N_MAPS = 2 * ATTN_HEADS
CONF_KERNEL = 31
SHORT_KERNEL = 3
POOL_WINDOWS = (2, 4, 8, 16)
GROUP_WIDTH = 256
SUBLANES = 8
LANES = 128
ANCHOR_ROWS = 16

ROW_TILE = 512
INPROJ_TILE = 1024
SUB_TILE = 256
ATTN_TILE = 256
CONV_CHUNK = 64
HALO = 32
VMEM_LIMIT = 56 * 1024 * 1024


def _split_bf16(x):
    hi = x.astype(BF16)
    lo = (x - hi.astype(F32)).astype(BF16)
    return hi, lo


def _group_sum(x, ones_blockdiag):
    return jnp.dot(x.astype(BF16), ones_blockdiag, preferred_element_type=F32)


def _modulated_norm(x, gain, shift, scale):
    ms = jnp.mean(x * x, axis=-1, keepdims=True)
    return (x * lax.rsqrt(ms + EPS)) * (gain * (1.0 + scale)) + shift


def _ada_kernel(c_ref, w_ref, b_ref, o_ref):
    c = c_ref[...]
    c_act = c * jax.nn.sigmoid(c)
    c_hi, c_lo = _split_bf16(c_act)
    w_hi, w_lo = _split_bf16(w_ref[0])
    acc = jnp.dot(c_hi, w_hi, preferred_element_type=F32)
    acc += jnp.dot(c_lo, w_hi, preferred_element_type=F32)
    acc += jnp.dot(c_hi, w_lo, preferred_element_type=F32)
    o_ref[0] = acc + b_ref[0]


def _ada_call(c, w_ada, b_ada):
    depth, d, n = w_ada.shape
    b = c.shape[0]
    tn = n // 4
    return pl.pallas_call(
        _ada_kernel,
        grid=(depth, n // tn),
        in_specs=[
            pl.BlockSpec((b, d), lambda l, j: (0, 0)),
            pl.BlockSpec((1, d, tn), lambda l, j: (l, 0, j)),
            pl.BlockSpec((1, 1, tn), lambda l, j: (l, 0, j)),
        ],
        out_specs=pl.BlockSpec((1, b, tn), lambda l, j: (l, 0, j)),
        out_shape=jax.ShapeDtypeStruct((depth, b, n), F32),
        compiler_params=pltpu.CompilerParams(
            dimension_semantics=("arbitrary", "arbitrary"), vmem_limit_bytes=VMEM_LIMIT),
        name="ada_mod",
    )(c, w_ada, b_ada.reshape(depth, 1, n))


def _conformer_pieces(h_scr, ph_scr, cw_ref, cbias_ref, lng_ref, lnb_ref, dst_ref):
    rows, w = dst_ref.shape
    first = HALO - (CONF_KERNEL - 1)
    phases = []
    for phase in range(SUBLANES):
        taps = [k for k in range(CONF_KERNEL) if (first + k) % SUBLANES == phase]
        src = (h_scr, first + taps[0]) if phase == 0 else (ph_scr.at[phase], 0)
        phases.append((taps,) + src)

    def shifted_copies():
        for phase in range(1, SUBLANES):
            taps = phases[phase][0]
            base = first + taps[0]
            span = taps[-1] - taps[0] + rows
            ph_scr[phase, 0:span, :] = h_scr[base:base + span, :]

    def chunk(r, bias=None):
        acc = jnp.broadcast_to(cbias_ref[...] if bias is None else bias, (CONV_CHUNK, w))
        for taps, src, base in phases:
            for k in taps:
                o = r + base + k - taps[0]
                acc = acc + cw_ref[k:k + 1, :] * src[o:o + CONV_CHUNK, :]
        mu = jnp.mean(acc, axis=-1, keepdims=True)
        cen = acc - mu
        var = jnp.mean(cen * cen, axis=-1, keepdims=True)
        y = cen * lax.rsqrt(var + EPS) * lng_ref[...] + lnb_ref[...]
        out = y * jax.nn.sigmoid(y)
        dst_ref[r:r + CONV_CHUNK, :] = out.astype(dst_ref.dtype)
        return out[0:ANCHOR_ROWS, 0:LANES]

    return [shifted_copies] + [functools.partial(chunk, r) for r in range(0, rows, CONV_CHUNK)]


def _inproj_mixers_kernel(x_ref, mod_ref, g_ref, w_ref, qkg_ref, ones32_ref, sw_ref, pw_ref, ps_ref,
                          qkv_ref, glu_ref, yo_ref,
                          hb_scr, pa_scr, pb_scr, c_scr, d_scr, e_scr, f_scr):
    ts = x_ref.shape[1]
    w = GROUP_WIDTH
    n_cols = w_ref.shape[2] // w
    rows = SUB_TILE
    j = pl.program_id(1)

    @pl.when(j == 0)
    def _():
        zeros = jnp.zeros((HALO, w), F32)
        c_scr[0:HALO, :] = zeros
        d_scr[0:HALO, :] = zeros

    lane = lax.broadcasted_iota(jnp.int32, (1, w), 1)
    group = lane // (w // len(POOL_WINDOWS))
    col = lambda c: slice(c * w, (c + 1) * w)

    def matmul_pieces(r0, dst):
        def normalise():
            hb_scr[...] = _modulated_norm(x_ref[0, r0:r0 + rows, :], g_ref[0], mod_ref[0, 0, 0:1, :],
                                          mod_ref[0, 0, 1:2, :]).astype(BF16)

        def project(c):
            dst[:, col(c)] = jnp.dot(hb_scr[...], w_ref[0, :, col(c)], preferred_element_type=F32)

        return [normalise] + [functools.partial(project, c) for c in range(n_cols)]

    def vector_pieces(r0, src):
        tile = slice(r0, r0 + rows)
        scr = slice(HALO + r0, HALO + r0 + rows)

        def qk_norm(c):
            p = src[:, col(c)]
            ms = _group_sum(p * p, ones32_ref[...]) * (1.0 / QK_DIM)
            p = p * lax.rsqrt(ms + EPS) * qkg_ref[0, :, col(c)]
            qkv_ref[0, tile, col(c)] = p.astype(qkv_ref.dtype)

        def v_and_glu():
            qkv_ref[0, tile, col(2)] = src[:, col(2)].astype(qkv_ref.dtype)
            glu_ref[0, tile, :] = src[:, col(3)] * jax.nn.sigmoid(src[:, col(4)])

        def short_conv():
            c_scr[scr, :] = src[:, col(6)] * src[:, col(7)]
            conv = None
            for k in range(SHORT_KERNEL):
                off = HALO + r0 - (SHORT_KERNEL - 1) + k
                term = sw_ref[0, k:k + 1, :] * c_scr[off:off + rows, :]
                conv = term if conv is None else conv + term
            yo_ref[0, tile, col(0)] = (src[:, col(5)] * conv).astype(yo_ref.dtype)

        def pool():
            d_scr[scr, :] = src[:, col(8)]
            end = HALO + r0 + rows
            lo = r0 + SUBLANES
            e_scr[lo:end, :] = d_scr[lo:end, :] + d_scr[lo - 1:end - 1, :]
            pooled = e_scr[scr, :]
            win = jnp.full((1, w), POOL_WINDOWS[0], jnp.int32)
            a, b = e_scr, f_scr
            for g in range(1, len(POOL_WINDOWS)):
                shift = POOL_WINDOWS[g - 1]
                lo = r0 + SUBLANES * (g + 1)
                b[lo:end, :] = a[lo:end, :] + a[lo - shift:end - shift, :]
                pooled = jnp.where(group >= g, b[scr, :], pooled)
                win = jnp.where(group >= g, POOL_WINDOWS[g], win)
                a, b = b, a
            t_pos = j * ts + r0 + lax.broadcasted_iota(jnp.int32, (rows, w), 0)
            cnt = jnp.minimum(t_pos + 1, win).astype(F32)
            y = (pooled / cnt - d_scr[scr, :]).astype(BF16)
            yd = jnp.dot(y, pw_ref[0], preferred_element_type=F32) * ps_ref[0]
            yo_ref[0, tile, col(1)] = yd.astype(yo_ref.dtype)

        return [functools.partial(qk_norm, 0), functools.partial(qk_norm, 1), v_and_glu,
                short_conv, pool]

    buffers = (pa_scr, pb_scr)
    n_sub = ts // rows
    for i in range(n_sub + 1):
        mm = matmul_pieces(i * rows, buffers[i % 2]) if i < n_sub else []
        vec = vector_pieces((i - 1) * rows, buffers[(i - 1) % 2]) if i > 0 else []
        stride = max(1, len(mm) // max(1, len(vec)))
        for k in range(max(len(mm), stride * len(vec))):
            if k < len(mm):
                mm[k]()
            if k % stride == stride - 1 and k // stride < len(vec):
                vec[k // stride]()

    for buf in (c_scr, d_scr):
        buf[0:HALO, :] = buf[ts:ts + HALO, :]


def _inproj_mixers_call(layer, x, mods, norm_g, w_in, qk_gain, ones32, short_w, pool_bd, pool_scale):
    b, s, d = x.shape
    n = w_in.shape[2]
    ts = INPROJ_TILE
    w = GROUP_WIDTH
    per_layer = lambda *shape: pl.BlockSpec((1,) + shape, lambda i, j: (layer,) + (0,) * len(shape))
    rows_of = lambda width: pl.BlockSpec((1, ts, width), lambda i, j: (i, j, 0))
    return pl.pallas_call(
        _inproj_mixers_kernel,
        grid=(b, s // ts),
        in_specs=[
            rows_of(d),
            pl.BlockSpec((1, 1, N_MOD, d), lambda i, j: (layer, i, 0, 0)),
            per_layer(1, d),
            per_layer(d, n),
            per_layer(1, 2 * w),
            pl.BlockSpec((w, w), lambda i, j: (0, 0)),
            per_layer(SHORT_KERNEL, w), per_layer(w, w), per_layer(1, w),
        ],
        out_specs=[rows_of(3 * w), rows_of(w), rows_of(2 * w)],
        out_shape=[jax.ShapeDtypeStruct((b, s, 3 * w), BF16),
                   jax.ShapeDtypeStruct((b, s, w), F32),
                   jax.ShapeDtypeStruct((b, s, 2 * w), BF16)],
        scratch_shapes=[pltpu.VMEM((SUB_TILE, d), BF16),
                        pltpu.VMEM((SUB_TILE, n), F32), pltpu.VMEM((SUB_TILE, n), F32)]
        + [pltpu.VMEM((HALO + ts, w), F32) for _ in range(4)],
        compiler_params=pltpu.CompilerParams(
            dimension_semantics=("arbitrary", "arbitrary"), vmem_limit_bytes=VMEM_LIMIT),
        name="inproj_mixers",
    )(x, mods, norm_g, w_in, qk_gain, ones32, short_w, pool_bd, pool_scale)


def _attn_kernel(q_ref, k_ref, v_ref, lamp_ref, subg_ref, o_ref,
                 vt_scr, qmt_scr, sa_scr, sb_scr, m_scr, acc_scr, ot_scr, *, lam_init):
    t = q_ref.shape[1]
    n_chunks = k_ref.shape[1] // t
    qi = pl.program_id(1)

    @pl.when(qi == 0)
    def _():
        for c in range(n_chunks):
            vt = v_ref[0, c * t:(c + 1) * t, :].astype(F32).T.astype(BF16)
            for head in range(ATTN_HEADS):
                r = head * VT_ROWS
                vt_scr[c, r:r + V_DIM, :] = vt[head * V_DIM:(head + 1) * V_DIM, :]
                vt_scr[c, r + V_DIM:r + VT_ROWS, :] = jnp.ones((VT_ROWS - V_DIM, t), BF16)

    qt = q_ref[0].astype(F32).T
    map_of_row = lax.broadcasted_iota(jnp.int32, qt.shape, 0) // QK_DIM
    for g in range(N_MAPS):
        qmt_scr[g] = jnp.where(map_of_row == g, qt, 0.0).astype(BF16)
    m_scr[...] = jnp.full(m_scr.shape, -jnp.inf, F32)
    acc_scr[...] = jnp.zeros(acc_scr.shape, F32)

    def fold(x, op):
        parts = [x[r:r + SUBLANES, :] for r in range(0, x.shape[0], SUBLANES)]
        while len(parts) > 1:
            parts = [op(parts[i], parts[i + 1]) for i in range(0, len(parts), 2)]
        return parts[0]

    def stage(score_chunk, score_dst, soft_chunk, soft_src, diagonal=False):
        if score_chunk is not None:
            kb = k_ref[0, pl.ds(pl.multiple_of(score_chunk * t, t), t), :]
        if diagonal:
            key = lax.broadcasted_iota(jnp.int32, (t, t), 0)
            query = lax.broadcasted_iota(jnp.int32, (t, t), 1)
            keep = key <= query
        for g in range(N_MAPS):
            if score_chunk is not None:
                score_dst[g] = jnp.dot(kb, qmt_scr[g], preferred_element_type=F32)
            if soft_chunk is None:
                continue
            head = g // 2
            st = soft_src[g]
            if diagonal:
                st = jnp.where(keep, st, -jnp.inf)
            m_prev = m_scr[g]
            m_new = jnp.maximum(m_prev, jnp.max(fold(st, jnp.maximum), axis=0, keepdims=True))
            alpha = jnp.exp2(m_prev - m_new)
            p = jnp.exp2((st - m_new).astype(BF16))
            m_scr[g] = m_new
            vt = vt_scr[soft_chunk, head * VT_ROWS:(head + 1) * VT_ROWS, :]
            pv = jnp.dot(vt, p, preferred_element_type=F32)
            acc_scr[g] = acc_scr[g] * alpha + pv

    stage(0, sa_scr, None, None)

    def body(i, carry):
        stage(2 * i + 1, sb_scr, 2 * i, sa_scr)
        stage(2 * i + 2, sa_scr, 2 * i + 1, sb_scr)
        return carry

    lax.fori_loop(0, qi // 2, body, 0)

    @pl.when(qi % 2 == 0)
    def _():
        stage(None, None, qi, sa_scr, diagonal=True)

    @pl.when(qi % 2 == 1)
    def _():
        stage(qi, sb_scr, qi - 1, sa_scr)
        stage(None, None, qi, sb_scr, diagonal=True)

    lp = lamp_ref[0]
    lam = (jnp.exp(jnp.sum(lp[0:1] * lp[1:2], axis=-1, keepdims=True))
           - jnp.exp(jnp.sum(lp[2:3] * lp[3:4], axis=-1, keepdims=True)) + lam_init)
    for head in range(ATTN_HEADS):
        o = []
        for mp in range(2):
            g = 2 * head + mp
            o.append(acc_scr[g, 0:V_DIM, :] / acc_scr[g, V_DIM:V_DIM + 1, :])
        out = o[0] - lam * o[1]
        ms = jnp.mean(out * out, axis=0, keepdims=True)
        out = out * lax.rsqrt(ms + EPS) * subg_ref[0] * (1.0 - lam_init)
        ot_scr[head * V_DIM:(head + 1) * V_DIM, :] = out
    o_ref[0] = ot_scr[...].T.astype(o_ref.dtype)


def _attn_call(layer, qkv, lam_params, sub_gain, lam_init):
    b, s, _ = qkv.shape
    t = ATTN_TILE
    w = GROUP_WIDTH
    return pl.pallas_call(
        functools.partial(_attn_kernel, lam_init=lam_init),
        grid=(b, s // t),
        in_specs=[
            pl.BlockSpec((1, t, w), lambda i, j: (i, j, 0)),
            pl.BlockSpec((1, s, w), lambda i, j: (i, 0, 1)),
            pl.BlockSpec((1, s, w), lambda i, j: (i, 0, 2)),
            pl.BlockSpec((1, 4, QK_DIM), lambda i, j: (layer, 0, 0)),
            pl.BlockSpec((1, V_DIM, t), lambda i, j: (layer, 0, 0)),
        ],
        out_specs=pl.BlockSpec((1, t, w), lambda i, j: (i, j, 0)),
        out_shape=jax.ShapeDtypeStruct((b, s, w), BF16),
        scratch_shapes=[
            pltpu.VMEM((s // t, ATTN_HEADS * VT_ROWS, t), BF16),
            pltpu.VMEM((N_MAPS, w, t), BF16),
            pltpu.VMEM((N_MAPS, t, t), F32),
            pltpu.VMEM((N_MAPS, t, t), F32),
            pltpu.VMEM((N_MAPS, 1, t), F32),
            pltpu.VMEM((N_MAPS, VT_ROWS, t), F32),
            pltpu.VMEM((w, t), F32),
        ],
        compiler_params=pltpu.CompilerParams(
            dimension_semantics=("arbitrary", "arbitrary"), vmem_limit_bytes=VMEM_LIMIT),
        name="diff_attn",
    )(qkv, qkv, qkv, lam_params, sub_gain)


def _outproj_mlp_kernel(x_ref, ya_ref, yo_ref, glu0_ref, glun_ref, mod_ref, g_ref,
                        wout_ref, w1_ref, w2_ref, cw_ref, cbias_ref, lng_ref, lnb_ref,
                        o_ref, x1_scr, hb_scr, a_scr, yb_scr, hw_scr, ph_scr):
    tm = x_ref.shape[1]
    w = GROUP_WIDTH
    d_ff = w1_ref.shape[2]
    j = pl.program_id(1)
    n_tiles = pl.num_programs(1)
    first_step = jnp.logical_and(pl.program_id(0) == 0, j == 0)
    mod = lambda r: mod_ref[0, 0, r:r + 1, :]

    def conformer_pieces():
        return _conformer_pieces(hw_scr, ph_scr, cw_ref.at[0], cbias_ref.at[0], lng_ref.at[0],
                                 lnb_ref.at[0], yb_scr)

    @pl.when(first_step)
    def _():
        hw_scr[0:HALO, :] = jnp.zeros((HALO, w), F32)
        hw_scr[HALO:, :] = glu0_ref[0]
        for piece in conformer_pieces():
            piece()
        hw_scr[0:HALO, :] = hw_scr[tm:, :]

    mix = jnp.dot(ya_ref[0], wout_ref[0, 0:w, :], preferred_element_type=F32)
    mix = mix + jnp.dot(yb_scr[...], wout_ref[0, w:2 * w, :], preferred_element_type=F32)
    mix = mix + jnp.dot(yo_ref[0], wout_ref[0, 2 * w:, :], preferred_element_type=F32)
    x1 = x_ref[0] + mod(2) * mix
    x1_scr[...] = x1
    hb_scr[...] = _modulated_norm(x1, g_ref[0], mod(3), mod(4)).astype(BF16)

    tail = hw_scr[0:HALO, :]
    hw_scr[0:HALO, :] = jnp.where(j == n_tiles - 1, 0.0, tail)
    hw_scr[HALO:, :] = glun_ref[0]
    pieces = conformer_pieces()
    pieces.pop(0)()

    always = j < n_tiles

    def anchor(ref, cols, slab):
        kept = ref[0:ANCHOR_ROWS, cols]
        ref[0:ANCHOR_ROWS, cols] = jnp.where(always, kept, slab.astype(kept.dtype))

    chunk = 1024
    n_chunks = d_ff // chunk
    per_chunk = len(pieces) // (2 * n_chunks)
    first_lanes = slice(0, LANES)

    def mixer_group(released_by, next_input, next_cols):
        bias = jnp.where(always, cbias_ref[0], released_by[0:1, 0:w])
        for _ in range(per_chunk):
            anchor(next_input, next_cols, pieces.pop(0)(bias))

    released_by = x1
    for ci in range(n_chunks):
        c = ci * chunk
        a = jnp.dot(hb_scr[...], w1_ref[0, :, c:c + chunk], preferred_element_type=F32)
        a_scr[:, c:c + chunk] = jnp.square(jnp.maximum(a, 0.0)).astype(BF16)
        if ci + 1 < n_chunks:
            mixer_group(released_by, hb_scr, first_lanes)
        else:
            mixer_group(released_by, a_scr, first_lanes)
        released_by = a
    y = None
    for ci in range(n_chunks):
        c = ci * chunk
        part = jnp.dot(a_scr[:, c:c + chunk], w2_ref[0, c:c + chunk, :], preferred_element_type=F32)
        y = part if y is None else y + part
        if ci + 1 < n_chunks:
            mixer_group(released_by, a_scr, slice(c + chunk, c + chunk + LANES))
        else:
            mixer_group(released_by, x1_scr, first_lanes)
        released_by = part
    for piece in pieces:
        piece()
    hw_scr[0:HALO, :] = hw_scr[tm:, :]
    o_ref[0] = x1_scr[...] + mod(5) * y


def _outproj_mlp_call(layer, x, ya, glu, yo, mods, norm_g, w_out, w1, w2, conf_w, conf_b, ln_g, ln_b):
    b, s, d = x.shape
    d_ff = w1.shape[2]
    tm = ROW_TILE
    w = GROUP_WIDTH
    n_tiles = s // tm
    resident = lambda arr: pl.BlockSpec((1,) + arr.shape[1:], lambda i, j: (layer, 0, 0),
                                        pipeline_mode=pl.Buffered(1))
    per_layer = lambda *shape: pl.BlockSpec((1,) + shape, lambda i, j: (layer,) + (0,) * len(shape))
    rows_of = lambda width: pl.BlockSpec((1, tm, width), lambda i, j: (i, j, 0))

    def next_tile(i, j):
        t = jnp.minimum(i * n_tiles + j + 1, b * n_tiles - 1)
        return (t // n_tiles, t % n_tiles, 0)

    return pl.pallas_call(
        _outproj_mlp_kernel,
        grid=(b, n_tiles),
        in_specs=[
            rows_of(d), rows_of(ya.shape[2]), rows_of(yo.shape[2]),
            pl.BlockSpec((1, tm, w), lambda i, j: (0, 0, 0)),
            pl.BlockSpec((1, tm, w), next_tile),
            pl.BlockSpec((1, 1, N_MOD, d), lambda i, j: (layer, i, 0, 0)),
            per_layer(1, d),
            resident(w_out), resident(w1), resident(w2),
            per_layer(CONF_KERNEL, w), per_layer(1, w), per_layer(1, w), per_layer(1, w),
        ],
        out_specs=rows_of(d),
        out_shape=jax.ShapeDtypeStruct((b, s, d), F32),
        scratch_shapes=[
            pltpu.VMEM((tm, d), F32),
            pltpu.VMEM((tm, d), BF16),
            pltpu.VMEM((tm, d_ff), BF16),
            pltpu.VMEM((tm, w), BF16),
            pltpu.VMEM((HALO + tm, w), F32),
            pltpu.VMEM((SUBLANES, HALO + tm, w), F32),
        ],
        compiler_params=pltpu.CompilerParams(
            dimension_semantics=("arbitrary", "arbitrary"), vmem_limit_bytes=VMEM_LIMIT),
        name="outproj_mlp",
    )(x, ya, yo, glu, glu, mods, norm_g, w_out, w1, w2, conf_w, conf_b, ln_g, ln_b)


def _block_diag_ones(width, group):
    idx = jnp.arange(width) // group
    return (idx[:, None] == idx[None, :]).astype(BF16)


def kernel(x, c, w_ada, b_ada, norm1_g, norm2_g, w_in, w_out, q_norm_g, k_norm_g, lam_params,
           attn_sub_g, conf_dw_w, conf_dw_b, conf_ln_g, conf_ln_b, short_conv_w, pool_w, pool_scale,
           w_ff1, w_ff2):
    depth, d, _ = w_in.shape
    b = x.shape[0]
    w = GROUP_WIDTH
    mods = _ada_call(c, w_ada, b_ada).reshape(depth, b, N_MOD, d)
    ones32 = _block_diag_ones(w, QK_DIM)
    rows = lambda v: v.reshape(depth, 1, -1)

    qk_gain = jnp.concatenate([jnp.tile(q_norm_g, (1, N_MAPS)) * (QK_DIM ** -0.5 * LOG2_E),
                               jnp.tile(k_norm_g, (1, N_MAPS))], axis=1).reshape(depth, 1, 2 * w)
    sub_gain = jnp.broadcast_to(attn_sub_g[:, :, None], (depth, V_DIM, ATTN_TILE))
    n_pool = len(POOL_WINDOWS)
    group_eye = jnp.eye(n_pool, dtype=pool_w.dtype)
    pool_bd = jnp.einsum('lgcd,gh->lgchd', pool_w, group_eye).reshape(depth, w, w).astype(BF16)
    w_in_b, w_out_b = w_in.astype(BF16), w_out.astype(BF16)
    w_ff1_b, w_ff2_b = w_ff1.astype(BF16), w_ff2.astype(BF16)

    for l in range(depth):
        lam_init = 0.8 - 0.6 * math.exp(-0.3 * l)
        qkv, glu, yo = _inproj_mixers_call(
            l, x, mods, rows(norm1_g), w_in_b, qk_gain, ones32,
            short_conv_w, pool_bd, rows(pool_scale))
        ya = _attn_call(l, qkv, lam_params, sub_gain, lam_init)
        x = _outproj_mlp_call(l, x, ya, glu, yo, mods, rows(norm2_g), w_out_b, w_ff1_b, w_ff2_b,
                              conf_dw_w, rows(conf_dw_b), rows(conf_ln_g), rows(conf_ln_b))
    return x
```

```python
import functools
import math

import jax
import jax.numpy as jnp
from jax import lax
from jax.experimental import pallas as pl
from jax.experimental.pallas import tpu as pltpu

F32 = jnp.float32
BF16 = jnp.bfloat16

EPS = 1e-6
LOG2_E = math.log2(math.e)
N_MOD = 6
ATTN_HEADS = 4
QK_DIM = 32
V_DIM = 64
VT_ROWS = V_DIM + 16
N_MAPS = 2 * ATTN_HEADS
CONF_KERNEL = 31
SHORT_KERNEL = 3
POOL_WINDOWS = (2, 4, 8, 16)
GROUP_WIDTH = 256
SUBLANES = 8
LANES = 128
ANCHOR_ROWS = 16

ROW_TILE = 512
INPROJ_TILE = 1024
SUB_TILE = 256
ATTN_TILE = 256
SCORE_LEAD = 2
CONV_CHUNK = 64
HALO = 32
VMEM_LIMIT = 56 * 1024 * 1024


def _split_bf16(x):
    hi = x.astype(BF16)
    lo = (x - hi.astype(F32)).astype(BF16)
    return hi, lo


def _group_sum(x, ones_blockdiag):
    return jnp.dot(x.astype(BF16), ones_blockdiag, preferred_element_type=F32)


def _modulated_norm(x, gain, shift, scale):
    ms = jnp.mean(x * x, axis=-1, keepdims=True)
    return (x * lax.rsqrt(ms + EPS)) * (gain * (1.0 + scale)) + shift


def _ada_kernel(c_ref, w_ref, b_ref, o_ref):
    c = c_ref[...]
    c_act = c * jax.nn.sigmoid(c)
    c_hi, c_lo = _split_bf16(c_act)
    w_hi, w_lo = _split_bf16(w_ref[0])
    acc = jnp.dot(c_hi, w_hi, preferred_element_type=F32)
    acc += jnp.dot(c_lo, w_hi, preferred_element_type=F32)
    acc += jnp.dot(c_hi, w_lo, preferred_element_type=F32)
    o_ref[0] = acc + b_ref[0]


def _ada_call(c, w_ada, b_ada):
    depth, d, n = w_ada.shape
    b = c.shape[0]
    tn = n // 4
    return pl.pallas_call(
        _ada_kernel,
        grid=(depth, n // tn),
        in_specs=[
            pl.BlockSpec((b, d), lambda l, j: (0, 0)),
            pl.BlockSpec((1, d, tn), lambda l, j: (l, 0, j)),
            pl.BlockSpec((1, 1, tn), lambda l, j: (l, 0, j)),
        ],
        out_specs=pl.BlockSpec((1, b, tn), lambda l, j: (l, 0, j)),
        out_shape=jax.ShapeDtypeStruct((depth, b, n), F32),
        compiler_params=pltpu.CompilerParams(
            dimension_semantics=("arbitrary", "arbitrary"), vmem_limit_bytes=VMEM_LIMIT),
        name="ada_mod",
    )(c, w_ada, b_ada.reshape(depth, 1, n))


def _conformer_pieces(h_scr, ph_scr, cw_ref, cbias_ref, lng_ref, lnb_ref, dst_ref):
    rows, w = dst_ref.shape
    first = HALO - (CONF_KERNEL - 1)
    phases = []
    for phase in range(SUBLANES):
        taps = [k for k in range(CONF_KERNEL) if (first + k) % SUBLANES == phase]
        src = (h_scr, first + taps[0]) if phase == 0 else (ph_scr.at[phase], 0)
        phases.append((taps,) + src)

    def shifted_copies():
        for phase in range(1, SUBLANES):
            taps = phases[phase][0]
            base = first + taps[0]
            span = taps[-1] - taps[0] + rows
            ph_scr[phase, 0:span, :] = h_scr[base:base + span, :]

    def chunk(r, bias=None):
        acc = jnp.broadcast_to(cbias_ref[...] if bias is None else bias, (CONV_CHUNK, w))
        for taps, src, base in phases:
            for k in taps:
                o = r + base + k - taps[0]
                acc = acc + cw_ref[k:k + 1, :] * src[o:o + CONV_CHUNK, :]
        mu = jnp.mean(acc, axis=-1, keepdims=True)
        cen = acc - mu
        var = jnp.mean(cen * cen, axis=-1, keepdims=True)
        y = cen * lax.rsqrt(var + EPS) * lng_ref[...] + lnb_ref[...]
        out = y * jax.nn.sigmoid(y)
        dst_ref[r:r + CONV_CHUNK, :] = out.astype(dst_ref.dtype)
        return out[0:ANCHOR_ROWS, 0:LANES]

    return [shifted_copies] + [functools.partial(chunk, r) for r in range(0, rows, CONV_CHUNK)]


def _inproj_mixers_kernel(x_ref, mod_ref, g_ref, w_ref, qkg_ref, ones32_ref, sw_ref, pw_ref, ps_ref,
                          qt_ref, kv_ref, glu_ref, yo_ref,
                          hb_scr, pa_scr, pb_scr, c_scr, d_scr, e_scr, f_scr):
    ts = x_ref.shape[1]
    w = GROUP_WIDTH
    n_cols = w_ref.shape[2] // w
    rows = SUB_TILE
    j = pl.program_id(1)

    @pl.when(j == 0)
    def _():
        zeros = jnp.zeros((HALO, w), F32)
        c_scr[0:HALO, :] = zeros
        d_scr[0:HALO, :] = zeros

    lane = lax.broadcasted_iota(jnp.int32, (1, w), 1)
    group = lane // (w // len(POOL_WINDOWS))
    col = lambda c: slice(c * w, (c + 1) * w)

    def matmul_pieces(r0, dst):
        def normalise():
            hb_scr[...] = _modulated_norm(x_ref[0, r0:r0 + rows, :], g_ref[0], mod_ref[0, 0, 0:1, :],
                                          mod_ref[0, 0, 1:2, :]).astype(BF16)

        def project(c):
            dst[:, col(c)] = jnp.dot(hb_scr[...], w_ref[0, :, col(c)], preferred_element_type=F32)

        return [normalise] + [functools.partial(project, c) for c in range(n_cols)]

    def vector_pieces(r0, src):
        tile = slice(r0, r0 + rows)
        scr = slice(HALO + r0, HALO + r0 + rows)

        def qk_norm(c):
            p = src[:, col(c)]
            ms = _group_sum(p * p, ones32_ref[...]) * (1.0 / QK_DIM)
            p = p * lax.rsqrt(ms + EPS) * qkg_ref[0, :, col(c)]
            if c == 0:
                qt_ref[0, :, tile] = p.T.astype(qt_ref.dtype)
            else:
                kv_ref[0, tile, col(0)] = p.astype(kv_ref.dtype)

        def v_and_glu():
            kv_ref[0, tile, col(1)] = src[:, col(2)].astype(kv_ref.dtype)
            glu_ref[0, tile, :] = src[:, col(3)] * jax.nn.sigmoid(src[:, col(4)])

        def short_conv():
            c_scr[scr, :] = src[:, col(6)] * src[:, col(7)]
            conv = None
            for k in range(SHORT_KERNEL):
                off = HALO + r0 - (SHORT_KERNEL - 1) + k
                term = sw_ref[0, k:k + 1, :] * c_scr[off:off + rows, :]
                conv = term if conv is None else conv + term
            yo_ref[0, tile, col(0)] = (src[:, col(5)] * conv).astype(yo_ref.dtype)

        def pool():
            d_scr[scr, :] = src[:, col(8)]
            end = HALO + r0 + rows
            lo = r0 + SUBLANES
            e_scr[lo:end, :] = d_scr[lo:end, :] + d_scr[lo - 1:end - 1, :]
            pooled = e_scr[scr, :]
            win = jnp.full((1, w), POOL_WINDOWS[0], jnp.int32)
            a, b = e_scr, f_scr
            for g in range(1, len(POOL_WINDOWS)):
                shift = POOL_WINDOWS[g - 1]
                lo = r0 + SUBLANES * (g + 1)
                b[lo:end, :] = a[lo:end, :] + a[lo - shift:end - shift, :]
                pooled = jnp.where(group >= g, b[scr, :], pooled)
                win = jnp.where(group >= g, POOL_WINDOWS[g], win)
                a, b = b, a
            t_pos = j * ts + r0 + lax.broadcasted_iota(jnp.int32, (rows, w), 0)
            cnt = jnp.minimum(t_pos + 1, win).astype(F32)
            y = (pooled / cnt - d_scr[scr, :]).astype(BF16)
            yd = jnp.dot(y, pw_ref[0], preferred_element_type=F32) * ps_ref[0]
            yo_ref[0, tile, col(1)] = yd.astype(yo_ref.dtype)

        return [functools.partial(qk_norm, 0), functools.partial(qk_norm, 1), v_and_glu,
                short_conv, pool]

    buffers = (pa_scr, pb_scr)
    n_sub = ts // rows
    for i in range(n_sub + 1):
        mm = matmul_pieces(i * rows, buffers[i % 2]) if i < n_sub else []
        vec = vector_pieces((i - 1) * rows, buffers[(i - 1) % 2]) if i > 0 else []
        stride = max(1, len(mm) // max(1, len(vec)))
        for k in range(max(len(mm), stride * len(vec))):
            if k < len(mm):
                mm[k]()
            if k % stride == stride - 1 and k // stride < len(vec):
                vec[k // stride]()

    for buf in (c_scr, d_scr):
        buf[0:HALO, :] = buf[ts:ts + HALO, :]


def _inproj_mixers_call(layer, x, mods, norm_g, w_in, qk_gain, ones32, short_w, pool_bd, pool_scale):
    b, s, d = x.shape
    n = w_in.shape[2]
    ts = INPROJ_TILE
    w = GROUP_WIDTH
    per_layer = lambda *shape: pl.BlockSpec((1,) + shape, lambda i, j: (layer,) + (0,) * len(shape))
    rows_of = lambda width: pl.BlockSpec((1, ts, width), lambda i, j: (i, j, 0))
    return pl.pallas_call(
        _inproj_mixers_kernel,
        grid=(b, s // ts),
        in_specs=[
            rows_of(d),
            pl.BlockSpec((1, 1, N_MOD, d), lambda i, j: (layer, i, 0, 0)),
            per_layer(1, d),
            per_layer(d, n),
            per_layer(1, 2 * w),
            pl.BlockSpec((w, w), lambda i, j: (0, 0)),
            per_layer(SHORT_KERNEL, w), per_layer(w, w), per_layer(1, w),
        ],
        out_specs=[pl.BlockSpec((1, w, ts), lambda i, j: (i, 0, j)),
                   rows_of(2 * w), rows_of(w), rows_of(2 * w)],
        out_shape=[jax.ShapeDtypeStruct((b, w, s), BF16),
                   jax.ShapeDtypeStruct((b, s, 2 * w), BF16),
                   jax.ShapeDtypeStruct((b, s, w), F32),
                   jax.ShapeDtypeStruct((b, s, 2 * w), BF16)],
        scratch_shapes=[pltpu.VMEM((SUB_TILE, d), BF16),
                        pltpu.VMEM((SUB_TILE, n), F32), pltpu.VMEM((SUB_TILE, n), F32)]
        + [pltpu.VMEM((HALO + ts, w), F32) for _ in range(4)],
        compiler_params=pltpu.CompilerParams(
            dimension_semantics=("arbitrary", "arbitrary"), vmem_limit_bytes=VMEM_LIMIT),
        name="inproj_mixers",
    )(x, mods, norm_g, w_in, qk_gain, ones32, short_w, pool_bd, pool_scale)


def _attn_kernel(q_ref, k_ref, v_ref, lamp_ref, subg_ref, o_ref,
                 vt_scr, qmt_scr, sa_scr, sb_scr, m_scr, acc_scr, ot_scr, *, lam_init):
    t = q_ref.shape[2]
    n_chunks = k_ref.shape[1] // t
    qi = pl.program_id(1)

    @pl.when(qi == 0)
    def _():
        for c in range(n_chunks):
            vt = v_ref[0, c * t:(c + 1) * t, :].astype(F32).T.astype(BF16)
            for head in range(ATTN_HEADS):
                r = head * VT_ROWS
                vt_scr[c, r:r + V_DIM, :] = vt[head * V_DIM:(head + 1) * V_DIM, :]
                vt_scr[c, r + V_DIM:r + VT_ROWS, :] = jnp.ones((VT_ROWS - V_DIM, t), BF16)
        qmt_scr[...] = jnp.zeros(qmt_scr.shape, BF16)

    for g in range(N_MAPS):
        rows = slice(g * QK_DIM, (g + 1) * QK_DIM)
        qmt_scr[g, rows, :] = q_ref[0, rows, :]
    m_scr[...] = jnp.full(m_scr.shape, -jnp.inf, F32)
    acc_scr[...] = jnp.zeros(acc_scr.shape, F32)

    def fold(x, op):
        parts = [x[r:r + SUBLANES, :] for r in range(0, x.shape[0], SUBLANES)]
        while len(parts) > 1:
            parts = [op(parts[i], parts[i + 1]) for i in range(0, len(parts), 2)]
        return parts[0]

    def stage(score_jobs, soft_chunk, soft_src, diagonal=False):
        keys = [k_ref[0, pl.ds(pl.multiple_of(chunk * t, t), t), :] for chunk, _ in score_jobs]
        if diagonal:
            key = lax.broadcasted_iota(jnp.int32, (t, t), 0)
            query = lax.broadcasted_iota(jnp.int32, (t, t), 1)
            keep = key <= query

        def scores(g):
            for kb, (_, dst) in zip(keys, score_jobs):
                dst[g] = jnp.dot(kb, qmt_scr[g], preferred_element_type=F32)

        for g in range(SCORE_LEAD):
            scores(g)
        for g in range(N_MAPS):
            if g + SCORE_LEAD < N_MAPS:
                scores(g + SCORE_LEAD)
            if soft_chunk is None:
                continue
            head = g // 2
            st = soft_src[g]
            if diagonal:
                st = jnp.where(keep, st, -jnp.inf)
            m_prev = m_scr[g]
            m_new = jnp.maximum(m_prev, jnp.max(fold(st, jnp.maximum), axis=0, keepdims=True))
            alpha = jnp.exp2(m_prev - m_new)
            p = jnp.exp2((st - m_new).astype(BF16))
            m_scr[g] = m_new
            vt = vt_scr[soft_chunk, head * VT_ROWS:(head + 1) * VT_ROWS, :]
            pv = jnp.dot(vt, p, preferred_element_type=F32)
            acc_scr[g] = acc_scr[g] * alpha + pv

    stage([(0, sa_scr)], None, None)

    def body(i, carry):
        stage([(2 * i + 1, sb_scr)], 2 * i, sa_scr)
        stage([(2 * i + 2, sa_scr)], 2 * i + 1, sb_scr)
        return carry

    lax.fori_loop(0, qi // 2, body, 0)

    @pl.when(qi % 2 == 0)
    def _():
        stage([], qi, sa_scr, diagonal=True)

    @pl.when(qi % 2 == 1)
    def _():
        stage([(qi, sb_scr)], qi - 1, sa_scr)
        stage([], qi, sb_scr, diagonal=True)

    lp = lamp_ref[0]
    lam = (jnp.exp(jnp.sum(lp[0:1] * lp[1:2], axis=-1, keepdims=True))
           - jnp.exp(jnp.sum(lp[2:3] * lp[3:4], axis=-1, keepdims=True)) + lam_init)
    for head in range(ATTN_HEADS):
        o = []
        for mp in range(2):
            g = 2 * head + mp
            o.append(acc_scr[g, 0:V_DIM, :] * (1.0 / acc_scr[g, V_DIM:V_DIM + 1, :]))
        out = o[0] - lam * o[1]
        ms = jnp.mean(out * out, axis=0, keepdims=True)
        out = out * lax.rsqrt(ms + EPS) * subg_ref[0] * (1.0 - lam_init)
        ot_scr[head * V_DIM:(head + 1) * V_DIM, :] = out
    o_ref[0] = ot_scr[...].T.astype(o_ref.dtype)


def _attn_call(layer, qt, kv, lam_params, sub_gain, lam_init):
    b, s, _ = kv.shape
    t = ATTN_TILE
    w = GROUP_WIDTH
    return pl.pallas_call(
        functools.partial(_attn_kernel, lam_init=lam_init),
        grid=(b, s // t),
        in_specs=[
            pl.BlockSpec((1, w, t), lambda i, j: (i, 0, j)),
            pl.BlockSpec((1, s, w), lambda i, j: (i, 0, 0)),
            pl.BlockSpec((1, s, w), lambda i, j: (i, 0, 1)),
            pl.BlockSpec((1, 4, QK_DIM), lambda i, j: (layer, 0, 0)),
            pl.BlockSpec((1, V_DIM, t), lambda i, j: (layer, 0, 0)),
        ],
        out_specs=pl.BlockSpec((1, t, w), lambda i, j: (i, j, 0)),
        out_shape=jax.ShapeDtypeStruct((b, s, w), BF16),
        scratch_shapes=[
            pltpu.VMEM((s // t, ATTN_HEADS * VT_ROWS, t), BF16),
            pltpu.VMEM((N_MAPS, w, t), BF16),
            pltpu.VMEM((N_MAPS, t, t), F32),
            pltpu.VMEM((N_MAPS, t, t), F32),
            pltpu.VMEM((N_MAPS, 1, t), F32),
            pltpu.VMEM((N_MAPS, VT_ROWS, t), F32),
            pltpu.VMEM((w, t), F32),
        ],
        compiler_params=pltpu.CompilerParams(
            dimension_semantics=("arbitrary", "arbitrary"), vmem_limit_bytes=VMEM_LIMIT),
        name="diff_attn",
    )(qt, kv, kv, lam_params, sub_gain)


def _outproj_mlp_kernel(x_ref, ya_ref, yo_ref, glu0_ref, glun_ref, mod_ref, g_ref,
                        wout_ref, w1_ref, w2_ref, cw_ref, cbias_ref, lng_ref, lnb_ref,
                        o_ref, x1_scr, hb_scr, a_scr, yb_scr, hw_scr, ph_scr):
    tm = x_ref.shape[1]
    w = GROUP_WIDTH
    d_ff = w1_ref.shape[2]
    j = pl.program_id(1)
    n_tiles = pl.num_programs(1)
    first_step = jnp.logical_and(pl.program_id(0) == 0, j == 0)
    mod = lambda r: mod_ref[0, 0, r:r + 1, :]

    def conformer_pieces():
        return _conformer_pieces(hw_scr, ph_scr, cw_ref.at[0], cbias_ref.at[0], lng_ref.at[0],
                                 lnb_ref.at[0], yb_scr)

    @pl.when(first_step)
    def _():
        hw_scr[0:HALO, :] = jnp.zeros((HALO, w), F32)
        hw_scr[HALO:, :] = glu0_ref[0]
        for piece in conformer_pieces():
            piece()
        hw_scr[0:HALO, :] = hw_scr[tm:, :]

    mix = jnp.dot(ya_ref[0], wout_ref[0, 0:w, :], preferred_element_type=F32)
    mix = mix + jnp.dot(yb_scr[...], wout_ref[0, w:2 * w, :], preferred_element_type=F32)
    mix = mix + jnp.dot(yo_ref[0], wout_ref[0, 2 * w:, :], preferred_element_type=F32)
    x1 = x_ref[0] + mod(2) * mix
    x1_scr[...] = x1
    hb_scr[...] = _modulated_norm(x1, g_ref[0], mod(3), mod(4)).astype(BF16)

    tail = hw_scr[0:HALO, :]
    hw_scr[0:HALO, :] = jnp.where(j == n_tiles - 1, 0.0, tail)
    hw_scr[HALO:, :] = glun_ref[0]
    pieces = conformer_pieces()
    pieces.pop(0)()

    always = j < n_tiles

    def anchor(ref, cols, slab):
        kept = ref[0:ANCHOR_ROWS, cols]
        ref[0:ANCHOR_ROWS, cols] = jnp.where(always, kept, slab.astype(kept.dtype))

    chunk = 1024
    n_chunks = d_ff // chunk
    per_chunk = len(pieces) // (2 * n_chunks)
    first_lanes = slice(0, LANES)

    def mixer_group(released_by, next_input, next_cols):
        bias = jnp.where(always, cbias_ref[0], released_by[0:1, 0:w])
        for _ in range(per_chunk):
            anchor(next_input, next_cols, pieces.pop(0)(bias))

    released_by = x1
    for ci in range(n_chunks):
        c = ci * chunk
        a = jnp.dot(hb_scr[...], w1_ref[0, :, c:c + chunk], preferred_element_type=F32)
        a_scr[:, c:c + chunk] = jnp.square(jnp.maximum(a, 0.0)).astype(BF16)
        if ci + 1 < n_chunks:
            mixer_group(released_by, hb_scr, first_lanes)
        else:
            mixer_group(released_by, a_scr, first_lanes)
        released_by = a
    y = None
    for ci in range(n_chunks):
        c = ci * chunk
        part = jnp.dot(a_scr[:, c:c + chunk], w2_ref[0, c:c + chunk, :], preferred_element_type=F32)
        y = part if y is None else y + part
        if ci + 1 < n_chunks:
            mixer_group(released_by, a_scr, slice(c + chunk, c + chunk + LANES))
        else:
            mixer_group(released_by, x1_scr, first_lanes)
        released_by = part
    for piece in pieces:
        piece()
    hw_scr[0:HALO, :] = hw_scr[tm:, :]
    o_ref[0] = x1_scr[...] + mod(5) * y


def _outproj_mlp_call(layer, x, ya, glu, yo, mods, norm_g, w_out, w1, w2, conf_w, conf_b, ln_g, ln_b):
    b, s, d = x.shape
    d_ff = w1.shape[2]
    tm = ROW_TILE
    w = GROUP_WIDTH
    n_tiles = s // tm
    resident = lambda arr: pl.BlockSpec((1,) + arr.shape[1:], lambda i, j: (layer, 0, 0),
                                        pipeline_mode=pl.Buffered(1))
    per_layer = lambda *shape: pl.BlockSpec((1,) + shape, lambda i, j: (layer,) + (0,) * len(shape))
    rows_of = lambda width: pl.BlockSpec((1, tm, width), lambda i, j: (i, j, 0))

    def next_tile(i, j):
        t = jnp.minimum(i * n_tiles + j + 1, b * n_tiles - 1)
        return (t // n_tiles, t % n_tiles, 0)

    return pl.pallas_call(
        _outproj_mlp_kernel,
        grid=(b, n_tiles),
        in_specs=[
            rows_of(d), rows_of(ya.shape[2]), rows_of(yo.shape[2]),
            pl.BlockSpec((1, tm, w), lambda i, j: (0, 0, 0)),
            pl.BlockSpec((1, tm, w), next_tile),
            pl.BlockSpec((1, 1, N_MOD, d), lambda i, j: (layer, i, 0, 0)),
            per_layer(1, d),
            resident(w_out), resident(w1), resident(w2),
            per_layer(CONF_KERNEL, w), per_layer(1, w), per_layer(1, w), per_layer(1, w),
        ],
        out_specs=rows_of(d),
        out_shape=jax.ShapeDtypeStruct((b, s, d), F32),
        scratch_shapes=[
            pltpu.VMEM((tm, d), F32),
            pltpu.VMEM((tm, d), BF16),
            pltpu.VMEM((tm, d_ff), BF16),
            pltpu.VMEM((tm, w), BF16),
            pltpu.VMEM((HALO + tm, w), F32),
            pltpu.VMEM((SUBLANES, HALO + tm, w), F32),
        ],
        compiler_params=pltpu.CompilerParams(
            dimension_semantics=("arbitrary", "arbitrary"), vmem_limit_bytes=VMEM_LIMIT),
        name="outproj_mlp",
    )(x, ya, yo, glu, glu, mods, norm_g, w_out, w1, w2, conf_w, conf_b, ln_g, ln_b)


def _block_diag_ones(width, group):
    idx = jnp.arange(width) // group
    return (idx[:, None] == idx[None, :]).astype(BF16)


def kernel(x, c, w_ada, b_ada, norm1_g, norm2_g, w_in, w_out, q_norm_g, k_norm_g, lam_params,
           attn_sub_g, conf_dw_w, conf_dw_b, conf_ln_g, conf_ln_b, short_conv_w, pool_w, pool_scale,
           w_ff1, w_ff2):
    depth, d, _ = w_in.shape
    b = x.shape[0]
    w = GROUP_WIDTH
    mods = _ada_call(c, w_ada, b_ada).reshape(depth, b, N_MOD, d)
    ones32 = _block_diag_ones(w, QK_DIM)
    rows = lambda v: v.reshape(depth, 1, -1)

    qk_gain = jnp.concatenate([jnp.tile(q_norm_g, (1, N_MAPS)) * (QK_DIM ** -0.5 * LOG2_E),
                               jnp.tile(k_norm_g, (1, N_MAPS))], axis=1).reshape(depth, 1, 2 * w)
    sub_gain = jnp.broadcast_to(attn_sub_g[:, :, None], (depth, V_DIM, ATTN_TILE))
    n_pool = len(POOL_WINDOWS)
    group_eye = jnp.eye(n_pool, dtype=pool_w.dtype)
    pool_bd = jnp.einsum('lgcd,gh->lgchd', pool_w, group_eye).reshape(depth, w, w).astype(BF16)
    w_in_b, w_out_b = w_in.astype(BF16), w_out.astype(BF16)
    w_ff1_b, w_ff2_b = w_ff1.astype(BF16), w_ff2.astype(BF16)

    for l in range(depth):
        lam_init = 0.8 - 0.6 * math.exp(-0.3 * l)
        qt, kv, glu, yo = _inproj_mixers_call(
            l, x, mods, rows(norm1_g), w_in_b, qk_gain, ones32,
            short_conv_w, pool_bd, rows(pool_scale))
        ya = _attn_call(l, qt, kv, lam_params, sub_gain, lam_init)
        x = _outproj_mlp_call(l, x, ya, glu, yo, mods, rows(norm2_g), w_out_b, w_ff1_b, w_ff2_b,
                              conf_dw_w, rows(conf_dw_b), rows(conf_ln_g), rows(conf_ln_b))
    return x
```

```python
import functools
import math

import jax
import jax.numpy as jnp
from jax import lax
from jax.experimental import pallas as pl
from jax.experimental.pallas import tpu as pltpu

F32 = jnp.float32
BF16 = jnp.bfloat16

EPS = 1e-6
LOG2_E = math.log2(math.e)
N_MOD = 6
ATTN_HEADS = 4
QK_DIM = 32
V_DIM = 64
VT_ROWS = V_DIM + 16
N_MAPS = 2 * ATTN_HEADS
CONF_KERNEL = 31
SHORT_KERNEL = 3
POOL_WINDOWS = (2, 4, 8, 16)
GROUP_WIDTH = 256
SUBLANES = 8
LANES = 128
ANCHOR_ROWS = 16

ROW_TILE = 512
INPROJ_TILE = 1024
SUB_TILE = 256
ATTN_TILE = 256
SCORE_LEAD = 2
CONV_CHUNK = 64
HALO = 32
VMEM_LIMIT = 56 * 1024 * 1024


def _split_bf16(x):
    hi = x.astype(BF16)
    lo = (x - hi.astype(F32)).astype(BF16)
    return hi, lo


def _group_sum(x, ones_blockdiag):
    return jnp.dot(x.astype(BF16), ones_blockdiag, preferred_element_type=F32)


def _modulated_norm(x, gain, shift, scale):
    ms = jnp.mean(x * x, axis=-1, keepdims=True)
    return (x * lax.rsqrt(ms + EPS)) * (gain * (1.0 + scale)) + shift


def _ada_kernel(c_ref, w_ref, b_ref, o_ref):
    c = c_ref[...]
    c_act = c * jax.nn.sigmoid(c)
    c_hi, c_lo = _split_bf16(c_act)
    w_hi, w_lo = _split_bf16(w_ref[0])
    acc = jnp.dot(c_hi, w_hi, preferred_element_type=F32)
    acc += jnp.dot(c_lo, w_hi, preferred_element_type=F32)
    acc += jnp.dot(c_hi, w_lo, preferred_element_type=F32)
    o_ref[0] = acc + b_ref[0]


def _ada_call(c, w_ada, b_ada):
    depth, d, n = w_ada.shape
    b = c.shape[0]
    tn = n // 4
    return pl.pallas_call(
        _ada_kernel,
        grid=(depth, n // tn),
        in_specs=[
            pl.BlockSpec((b, d), lambda l, j: (0, 0)),
            pl.BlockSpec((1, d, tn), lambda l, j: (l, 0, j)),
            pl.BlockSpec((1, 1, tn), lambda l, j: (l, 0, j)),
        ],
        out_specs=pl.BlockSpec((1, b, tn), lambda l, j: (l, 0, j)),
        out_shape=jax.ShapeDtypeStruct((depth, b, n), F32),
        compiler_params=pltpu.CompilerParams(
            dimension_semantics=("arbitrary", "arbitrary"), vmem_limit_bytes=VMEM_LIMIT),
        name="ada_mod",
    )(c, w_ada, b_ada.reshape(depth, 1, n))


def _conformer_pieces(h_scr, ph_scr, cw_ref, cbias_ref, lng_ref, lnb_ref, dst_ref):
    rows, w = dst_ref.shape
    first = HALO - (CONF_KERNEL - 1)
    phases = []
    for phase in range(SUBLANES):
        taps = [k for k in range(CONF_KERNEL) if (first + k) % SUBLANES == phase]
        src = (h_scr, first + taps[0]) if phase == 0 else (ph_scr.at[phase], 0)
        phases.append((taps,) + src)

    def shifted_copies():
        for phase in range(1, SUBLANES):
            taps = phases[phase][0]
            base = first + taps[0]
            span = taps[-1] - taps[0] + rows
            ph_scr[phase, 0:span, :] = h_scr[base:base + span, :]

    def chunk(r, bias=None):
        acc = jnp.broadcast_to(cbias_ref[...] if bias is None else bias, (CONV_CHUNK, w))
        for taps, src, base in phases:
            window = src[r + base:r + base + taps[-1] - taps[0] + CONV_CHUNK, :]
            for k in taps:
                o = k - taps[0]
                acc = acc + cw_ref[k:k + 1, :] * window[o:o + CONV_CHUNK, :]
        mu = jnp.mean(acc, axis=-1, keepdims=True)
        cen = acc - mu
        var = jnp.mean(cen * cen, axis=-1, keepdims=True)
        y = cen * lax.rsqrt(var + EPS) * lng_ref[...] + lnb_ref[...]
        out = y * jax.nn.sigmoid(y)
        dst_ref[r:r + CONV_CHUNK, :] = out.astype(dst_ref.dtype)
        return out[0:ANCHOR_ROWS, 0:LANES]

    return [shifted_copies] + [functools.partial(chunk, r) for r in range(0, rows, CONV_CHUNK)]


def _inproj_mixers_kernel(x_ref, mod_ref, g_ref, w_ref, qkg_ref, ones32_ref, sw_ref, pw_ref, ps_ref,
                          qt_ref, kv_ref, glu_ref, yo_ref,
                          hb_scr, pa_scr, pb_scr, c_scr, d_scr, e_scr, f_scr):
    ts = x_ref.shape[1]
    w = GROUP_WIDTH
    n_cols = w_ref.shape[2] // w
    rows = SUB_TILE
    j = pl.program_id(1)

    @pl.when(j == 0)
    def _():
        zeros = jnp.zeros((HALO, w), F32)
        c_scr[0:HALO, :] = zeros
        d_scr[0:HALO, :] = zeros

    lane = lax.broadcasted_iota(jnp.int32, (1, w), 1)
    group = lane // (w // len(POOL_WINDOWS))
    col = lambda c: slice(c * w, (c + 1) * w)

    def matmul_pieces(r0, dst):
        def normalise():
            hb_scr[...] = _modulated_norm(x_ref[0, r0:r0 + rows, :], g_ref[0], mod_ref[0, 0, 0:1, :],
                                          mod_ref[0, 0, 1:2, :]).astype(BF16)

        def project(c):
            dst[:, col(c)] = jnp.dot(hb_scr[...], w_ref[0, :, col(c)], preferred_element_type=F32)

        return [normalise] + [functools.partial(project, c) for c in range(n_cols)]

    def vector_pieces(r0, src):
        tile = slice(r0, r0 + rows)
        scr = slice(HALO + r0, HALO + r0 + rows)

        def qk_norm(c):
            p = src[:, col(c)]
            ms = _group_sum(p * p, ones32_ref[...]) * (1.0 / QK_DIM)
            p = p * lax.rsqrt(ms + EPS) * qkg_ref[0, :, col(c)]
            if c == 0:
                qt_ref[0, :, tile] = p.T.astype(qt_ref.dtype)
            else:
                kv_ref[0, tile, col(0)] = p.astype(kv_ref.dtype)

        def v_and_glu():
            kv_ref[0, tile, col(1)] = src[:, col(2)].astype(kv_ref.dtype)
            glu_ref[0, tile, :] = src[:, col(3)] * jax.nn.sigmoid(src[:, col(4)])

        def short_conv():
            c_scr[scr, :] = src[:, col(6)] * src[:, col(7)]
            conv = None
            for k in range(SHORT_KERNEL):
                off = HALO + r0 - (SHORT_KERNEL - 1) + k
                term = sw_ref[0, k:k + 1, :] * c_scr[off:off + rows, :]
                conv = term if conv is None else conv + term
            yo_ref[0, tile, col(0)] = (src[:, col(5)] * conv).astype(yo_ref.dtype)

        def pool():
            d_scr[scr, :] = src[:, col(8)]
            end = HALO + r0 + rows
            lo = r0 + SUBLANES
            e_scr[lo:end, :] = d_scr[lo:end, :] + d_scr[lo - 1:end - 1, :]
            pooled = e_scr[scr, :]
            win = jnp.full((1, w), POOL_WINDOWS[0], jnp.int32)
            a, b = e_scr, f_scr
            for g in range(1, len(POOL_WINDOWS)):
                shift = POOL_WINDOWS[g - 1]
                lo = r0 + SUBLANES * (g + 1)
                b[lo:end, :] = a[lo:end, :] + a[lo - shift:end - shift, :]
                pooled = jnp.where(group >= g, b[scr, :], pooled)
                win = jnp.where(group >= g, POOL_WINDOWS[g], win)
                a, b = b, a
            t_pos = j * ts + r0 + lax.broadcasted_iota(jnp.int32, (rows, w), 0)
            cnt = jnp.minimum(t_pos + 1, win).astype(F32)
            y = (pooled / cnt - d_scr[scr, :]).astype(BF16)
            yd = jnp.dot(y, pw_ref[0], preferred_element_type=F32) * ps_ref[0]
            yo_ref[0, tile, col(1)] = yd.astype(yo_ref.dtype)

        return [functools.partial(qk_norm, 0), functools.partial(qk_norm, 1), v_and_glu,
                short_conv, pool]

    buffers = (pa_scr, pb_scr)
    n_sub = ts // rows
    for i in range(n_sub + 1):
        mm = matmul_pieces(i * rows, buffers[i % 2]) if i < n_sub else []
        vec = vector_pieces((i - 1) * rows, buffers[(i - 1) % 2]) if i > 0 else []
        stride = max(1, len(mm) // max(1, len(vec)))
        for k in range(max(len(mm), stride * len(vec))):
            if k < len(mm):
                mm[k]()
            if k % stride == stride - 1 and k // stride < len(vec):
                vec[k // stride]()

    for buf in (c_scr, d_scr):
        buf[0:HALO, :] = buf[ts:ts + HALO, :]


def _inproj_mixers_call(layer, x, mods, norm_g, w_in, qk_gain, ones32, short_w, pool_bd, pool_scale):
    b, s, d = x.shape
    n = w_in.shape[2]
    ts = INPROJ_TILE
    w = GROUP_WIDTH
    per_layer = lambda *shape: pl.BlockSpec((1,) + shape, lambda i, j: (layer,) + (0,) * len(shape))
    rows_of = lambda width: pl.BlockSpec((1, ts, width), lambda i, j: (i, j, 0))
    return pl.pallas_call(
        _inproj_mixers_kernel,
        grid=(b, s // ts),
        in_specs=[
            rows_of(d),
            pl.BlockSpec((1, 1, N_MOD, d), lambda i, j: (layer, i, 0, 0)),
            per_layer(1, d),
            per_layer(d, n),
            per_layer(1, 2 * w),
            pl.BlockSpec((w, w), lambda i, j: (0, 0)),
            per_layer(SHORT_KERNEL, w), per_layer(w, w), per_layer(1, w),
        ],
        out_specs=[pl.BlockSpec((1, w, ts), lambda i, j: (i, 0, j)),
                   rows_of(2 * w), rows_of(w), rows_of(2 * w)],
        out_shape=[jax.ShapeDtypeStruct((b, w, s), BF16),
                   jax.ShapeDtypeStruct((b, s, 2 * w), BF16),
                   jax.ShapeDtypeStruct((b, s, w), F32),
                   jax.ShapeDtypeStruct((b, s, 2 * w), BF16)],
        scratch_shapes=[pltpu.VMEM((SUB_TILE, d), BF16),
                        pltpu.VMEM((SUB_TILE, n), F32), pltpu.VMEM((SUB_TILE, n), F32)]
        + [pltpu.VMEM((HALO + ts, w), F32) for _ in range(4)],
        compiler_params=pltpu.CompilerParams(
            dimension_semantics=("arbitrary", "arbitrary"), vmem_limit_bytes=VMEM_LIMIT),
        name="inproj_mixers",
    )(x, mods, norm_g, w_in, qk_gain, ones32, short_w, pool_bd, pool_scale)


def _attn_kernel(q_ref, qn_ref, k_ref, v_ref, lamp_ref, subg_ref, o_ref,
                 vt_scr, qmt_scr, qmn_scr, sa_scr, sb_scr, m_scr, acc_scr, ot_scr, *, lam_init):
    t = q_ref.shape[2]
    n_chunks = k_ref.shape[1] // t
    qi = pl.program_id(1)

    @pl.when(qi == 0)
    def _():
        for c in range(n_chunks):
            vt = v_ref[0, c * t:(c + 1) * t, :].astype(F32).T.astype(BF16)
            for head in range(ATTN_HEADS):
                r = head * VT_ROWS
                vt_scr[c, r:r + V_DIM, :] = vt[head * V_DIM:(head + 1) * V_DIM, :]
                vt_scr[c, r + V_DIM:r + VT_ROWS, :] = jnp.ones((VT_ROWS - V_DIM, t), BF16)
        qmt_scr[...] = jnp.zeros(qmt_scr.shape, BF16)
        qmn_scr[...] = jnp.zeros(qmn_scr.shape, BF16)

    for g in range(N_MAPS):
        rows = slice(g * QK_DIM, (g + 1) * QK_DIM)
        qmt_scr[g, rows, :] = q_ref[0, rows, :]
        qmn_scr[g, rows, :] = qn_ref[0, rows, :]
    m_scr[...] = jnp.full(m_scr.shape, -jnp.inf, F32)
    acc_scr[...] = jnp.zeros(acc_scr.shape, F32)

    def fold(x, op):
        parts = [x[r:r + SUBLANES, :] for r in range(0, x.shape[0], SUBLANES)]
        while len(parts) > 1:
            parts = [op(parts[i], parts[i + 1]) for i in range(0, len(parts), 2)]
        return parts[0]

    def stage(score_jobs, soft_chunk, soft_src, diagonal=False):
        keys = [k_ref[0, pl.ds(pl.multiple_of(job[0] * t, t), t), :] for job in score_jobs]
        if diagonal:
            key = lax.broadcasted_iota(jnp.int32, (t, t), 0)
            query = lax.broadcasted_iota(jnp.int32, (t, t), 1)
            keep = key <= query

        def scores(g):
            for kb, (_, qm, dst) in zip(keys, score_jobs):
                dst[g] = jnp.dot(kb, qm[g], preferred_element_type=F32)

        for g in range(SCORE_LEAD):
            scores(g)
        for g in range(N_MAPS):
            if g + SCORE_LEAD < N_MAPS:
                scores(g + SCORE_LEAD)
            if soft_chunk is None:
                continue
            head = g // 2
            st = soft_src[g]
            if diagonal:
                st = jnp.where(keep, st, -jnp.inf)
            m_prev = m_scr[g]
            m_new = jnp.maximum(m_prev, jnp.max(fold(st, jnp.maximum), axis=0, keepdims=True))
            alpha = jnp.exp2(m_prev - m_new)
            p = jnp.exp2((st - m_new).astype(BF16))
            m_scr[g] = m_new
            vt = vt_scr[soft_chunk, head * VT_ROWS:(head + 1) * VT_ROWS, :]
            pv = jnp.dot(vt, p, preferred_element_type=F32)
            acc_scr[g] = acc_scr[g] * alpha + pv

    def run(first, second):
        @pl.when(qi == 0)
        def _():
            stage([(0, qmt_scr, first)], None, None)

        def body(i, carry):
            stage([(2 * i + 1, qmt_scr, second)], 2 * i, first)
            stage([(2 * i + 2, qmt_scr, first)], 2 * i + 1, second)
            return carry

        lax.fori_loop(0, qi // 2, body, 0)

        @pl.when(qi % 2 == 0)
        def _():
            stage([(0, qmn_scr, second)], qi, first, diagonal=True)

        @pl.when(qi % 2 == 1)
        def _():
            stage([(qi, qmt_scr, second)], qi - 1, first)
            stage([(0, qmn_scr, first)], qi, second, diagonal=True)

    arrives_in_b = ((qi + 1) // 2) % 2

    @pl.when(arrives_in_b == 0)
    def _():
        run(sa_scr, sb_scr)

    @pl.when(arrives_in_b == 1)
    def _():
        run(sb_scr, sa_scr)

    lp = lamp_ref[0]
    lam = (jnp.exp(jnp.sum(lp[0:1] * lp[1:2], axis=-1, keepdims=True))
           - jnp.exp(jnp.sum(lp[2:3] * lp[3:4], axis=-1, keepdims=True)) + lam_init)
    for head in range(ATTN_HEADS):
        o = []
        for mp in range(2):
            g = 2 * head + mp
            o.append(acc_scr[g, 0:V_DIM, :] * (1.0 / acc_scr[g, V_DIM:V_DIM + 1, :]))
        out = o[0] - lam * o[1]
        ms = jnp.mean(out * out, axis=0, keepdims=True)
        out = out * lax.rsqrt(ms + EPS) * subg_ref[0] * (1.0 - lam_init)
        ot_scr[head * V_DIM:(head + 1) * V_DIM, :] = out
    o_ref[0] = ot_scr[...].T.astype(o_ref.dtype)


def _attn_call(layer, qt, kv, lam_params, sub_gain, lam_init):
    b, s, _ = kv.shape
    t = ATTN_TILE
    w = GROUP_WIDTH
    return pl.pallas_call(
        functools.partial(_attn_kernel, lam_init=lam_init),
        grid=(b, s // t),
        in_specs=[
            pl.BlockSpec((1, w, t), lambda i, j: (i, 0, j)),
            pl.BlockSpec((1, w, t), lambda i, j: (i, 0, jnp.minimum(j + 1, s // t - 1))),
            pl.BlockSpec((1, s, w), lambda i, j: (i, 0, 0)),
            pl.BlockSpec((1, s, w), lambda i, j: (i, 0, 1)),
            pl.BlockSpec((1, 4, QK_DIM), lambda i, j: (layer, 0, 0)),
            pl.BlockSpec((1, V_DIM, t), lambda i, j: (layer, 0, 0)),
        ],
        out_specs=pl.BlockSpec((1, t, w), lambda i, j: (i, j, 0)),
        out_shape=jax.ShapeDtypeStruct((b, s, w), BF16),
        scratch_shapes=[
            pltpu.VMEM((s // t, ATTN_HEADS * VT_ROWS, t), BF16),
            pltpu.VMEM((N_MAPS, w, t), BF16),
            pltpu.VMEM((N_MAPS, w, t), BF16),
            pltpu.VMEM((N_MAPS, t, t), F32),
            pltpu.VMEM((N_MAPS, t, t), F32),
            pltpu.VMEM((N_MAPS, 1, t), F32),
            pltpu.VMEM((N_MAPS, VT_ROWS, t), F32),
            pltpu.VMEM((w, t), F32),
        ],
        compiler_params=pltpu.CompilerParams(
            dimension_semantics=("arbitrary", "arbitrary"), vmem_limit_bytes=VMEM_LIMIT),
        name="diff_attn",
    )(qt, qt, kv, kv, lam_params, sub_gain)


def _outproj_mlp_kernel(x_ref, ya_ref, yo_ref, glu0_ref, glun_ref, mod_ref, g_ref,
                        wout_ref, w1_ref, w2_ref, cw_ref, cbias_ref, lng_ref, lnb_ref,
                        o_ref, x1_scr, hb_scr, a_scr, yb_scr, hw_scr, ph_scr):
    tm = x_ref.shape[1]
    w = GROUP_WIDTH
    d_ff = w1_ref.shape[2]
    j = pl.program_id(1)
    n_tiles = pl.num_programs(1)
    first_step = jnp.logical_and(pl.program_id(0) == 0, j == 0)
    mod = lambda r: mod_ref[0, 0, r:r + 1, :]

    def conformer_pieces():
        return _conformer_pieces(hw_scr, ph_scr, cw_ref.at[0], cbias_ref.at[0], lng_ref.at[0],
                                 lnb_ref.at[0], yb_scr)

    @pl.when(first_step)
    def _():
        hw_scr[0:HALO, :] = jnp.zeros((HALO, w), F32)
        hw_scr[HALO:, :] = glu0_ref[0]
        for piece in conformer_pieces():
            piece()
        hw_scr[0:HALO, :] = hw_scr[tm:, :]

    mix = jnp.dot(ya_ref[0], wout_ref[0, 0:w, :], preferred_element_type=F32)
    mix = mix + jnp.dot(yb_scr[...], wout_ref[0, w:2 * w, :], preferred_element_type=F32)
    mix = mix + jnp.dot(yo_ref[0], wout_ref[0, 2 * w:, :], preferred_element_type=F32)
    x1 = x_ref[0] + mod(2) * mix
    x1_scr[...] = x1
    hb_scr[...] = _modulated_norm(x1, g_ref[0], mod(3), mod(4)).astype(BF16)

    tail = hw_scr[0:HALO, :]
    hw_scr[0:HALO, :] = jnp.where(j == n_tiles - 1, 0.0, tail)
    hw_scr[HALO:, :] = glun_ref[0]
    pieces = conformer_pieces()
    pieces.pop(0)()

    always = j < n_tiles

    def anchor(ref, cols, slab):
        kept = ref[0:ANCHOR_ROWS, cols]
        ref[0:ANCHOR_ROWS, cols] = jnp.where(always, kept, slab.astype(kept.dtype))

    chunk = 1024
    n_chunks = d_ff // chunk
    per_chunk = len(pieces) // (2 * n_chunks)
    first_lanes = slice(0, LANES)

    def mixer_group(released_by, next_input, next_cols):
        bias = jnp.where(always, cbias_ref[0], released_by[0:1, 0:w])
        for _ in range(per_chunk):
            anchor(next_input, next_cols, pieces.pop(0)(bias))

    released_by = x1
    for ci in range(n_chunks):
        c = ci * chunk
        a = jnp.dot(hb_scr[...], w1_ref[0, :, c:c + chunk], preferred_element_type=F32)
        a_scr[:, c:c + chunk] = jnp.square(jnp.maximum(a, 0.0)).astype(BF16)
        if ci + 1 < n_chunks:
            mixer_group(released_by, hb_scr, first_lanes)
        else:
            mixer_group(released_by, a_scr, first_lanes)
        released_by = a
    y = None
    for ci in range(n_chunks):
        c = ci * chunk
        part = jnp.dot(a_scr[:, c:c + chunk], w2_ref[0, c:c + chunk, :], preferred_element_type=F32)
        y = part if y is None else y + part
        if ci + 1 < n_chunks:
            mixer_group(released_by, a_scr, slice(c + chunk, c + chunk + LANES))
        else:
            mixer_group(released_by, x1_scr, first_lanes)
        released_by = part
    for piece in pieces:
        piece()
    hw_scr[0:HALO, :] = hw_scr[tm:, :]
    o_ref[0] = x1_scr[...] + mod(5) * y


def _outproj_mlp_call(layer, x, ya, glu, yo, mods, norm_g, w_out, w1, w2, conf_w, conf_b, ln_g, ln_b):
    b, s, d = x.shape
    d_ff = w1.shape[2]
    tm = ROW_TILE
    w = GROUP_WIDTH
    n_tiles = s // tm
    resident = lambda arr: pl.BlockSpec((1,) + arr.shape[1:], lambda i, j: (layer, 0, 0),
                                        pipeline_mode=pl.Buffered(1))
    per_layer = lambda *shape: pl.BlockSpec((1,) + shape, lambda i, j: (layer,) + (0,) * len(shape))
    rows_of = lambda width: pl.BlockSpec((1, tm, width), lambda i, j: (i, j, 0))

    def next_tile(i, j):
        t = jnp.minimum(i * n_tiles + j + 1, b * n_tiles - 1)
        return (t // n_tiles, t % n_tiles, 0)

    return pl.pallas_call(
        _outproj_mlp_kernel,
        grid=(b, n_tiles),
        in_specs=[
            rows_of(d), rows_of(ya.shape[2]), rows_of(yo.shape[2]),
            pl.BlockSpec((1, tm, w), lambda i, j: (0, 0, 0)),
            pl.BlockSpec((1, tm, w), next_tile),
            pl.BlockSpec((1, 1, N_MOD, d), lambda i, j: (layer, i, 0, 0)),
            per_layer(1, d),
            resident(w_out), resident(w1), resident(w2),
            per_layer(CONF_KERNEL, w), per_layer(1, w), per_layer(1, w), per_layer(1, w),
        ],
        out_specs=rows_of(d),
        out_shape=jax.ShapeDtypeStruct((b, s, d), F32),
        scratch_shapes=[
            pltpu.VMEM((tm, d), F32),
            pltpu.VMEM((tm, d), BF16),
            pltpu.VMEM((tm, d_ff), BF16),
            pltpu.VMEM((tm, w), BF16),
            pltpu.VMEM((HALO + tm, w), F32),
            pltpu.VMEM((SUBLANES, HALO + tm, w), F32),
        ],
        compiler_params=pltpu.CompilerParams(
            dimension_semantics=("arbitrary", "arbitrary"), vmem_limit_bytes=VMEM_LIMIT),
        name="outproj_mlp",
    )(x, ya, yo, glu, glu, mods, norm_g, w_out, w1, w2, conf_w, conf_b, ln_g, ln_b)


def _block_diag_ones(width, group):
    idx = jnp.arange(width) // group
    return (idx[:, None] == idx[None, :]).astype(BF16)


def kernel(x, c, w_ada, b_ada, norm1_g, norm2_g, w_in, w_out, q_norm_g, k_norm_g, lam_params,
           attn_sub_g, conf_dw_w, conf_dw_b, conf_ln_g, conf_ln_b, short_conv_w, pool_w, pool_scale,
           w_ff1, w_ff2):
    depth, d, _ = w_in.shape
    b = x.shape[0]
    w = GROUP_WIDTH
    mods = _ada_call(c, w_ada, b_ada).reshape(depth, b, N_MOD, d)
    ones32 = _block_diag_ones(w, QK_DIM)
    rows = lambda v: v.reshape(depth, 1, -1)

    qk_gain = jnp.concatenate([jnp.tile(q_norm_g, (1, N_MAPS)) * (QK_DIM ** -0.5 * LOG2_E),
                               jnp.tile(k_norm_g, (1, N_MAPS))], axis=1).reshape(depth, 1, 2 * w)
    sub_gain = jnp.broadcast_to(attn_sub_g[:, :, None], (depth, V_DIM, ATTN_TILE))
    n_pool = len(POOL_WINDOWS)
    group_eye = jnp.eye(n_pool, dtype=pool_w.dtype)
    pool_bd = jnp.einsum('lgcd,gh->lgchd', pool_w, group_eye).reshape(depth, w, w).astype(BF16)
    w_in_b, w_out_b = w_in.astype(BF16), w_out.astype(BF16)
    w_ff1_b, w_ff2_b = w_ff1.astype(BF16), w_ff2.astype(BF16)

    for l in range(depth):
        lam_init = 0.8 - 0.6 * math.exp(-0.3 * l)
        qt, kv, glu, yo = _inproj_mixers_call(
            l, x, mods, rows(norm1_g), w_in_b, qk_gain, ones32,
            short_conv_w, pool_bd, rows(pool_scale))
        ya = _attn_call(l, qt, kv, lam_params, sub_gain, lam_init)
        x = _outproj_mlp_call(l, x, ya, glu, yo, mods, rows(norm2_g), w_out_b, w_ff1_b, w_ff2_b,
                              conf_dw_w, rows(conf_dw_b), rows(conf_ln_g), rows(conf_ln_b))
    return x
```

```python
import functools
import math

import jax
import jax.numpy as jnp
from jax import lax
from jax.experimental import pallas as pl
from jax.experimental.pallas import tpu as pltpu

F32 = jnp.float32
BF16 = jnp.bfloat16

EPS = 1e-6
LOG2_E = math.log2(math.e)
N_MOD = 6
ATTN_HEADS = 4
QK_DIM = 32
V_DIM = 64
VT_ROWS = V_DIM + 16
N_MAPS = 2 * ATTN_HEADS
CONF_KERNEL = 31
SHORT_KERNEL = 3
POOL_WINDOWS = (2, 4, 8, 16)
GROUP_WIDTH = 256
SUBLANES = 8
LANES = 128
ANCHOR_ROWS = 16

ROW_TILE = 512
INPROJ_TILE = 1024
SUB_TILE = 256
ATTN_TILE = 256
SCORE_LEAD = 2
CONV_CHUNK = 64
HALO = 32
VMEM_LIMIT = 56 * 1024 * 1024


def _split_bf16(x):
    hi = x.astype(BF16)
    lo = (x - hi.astype(F32)).astype(BF16)
    return hi, lo


def _group_sum(x, ones_blockdiag):
    return jnp.dot(x.astype(BF16), ones_blockdiag, preferred_element_type=F32)


def _modulated_norm(x, gain, shift, scale):
    ms = jnp.mean(x * x, axis=-1, keepdims=True)
    return (x * lax.rsqrt(ms + EPS)) * (gain * (1.0 + scale)) + shift


def _ada_kernel(c_ref, w_ref, b_ref, o_ref):
    c = c_ref[...]
    c_act = c * jax.nn.sigmoid(c)
    c_hi, c_lo = _split_bf16(c_act)
    w_hi, w_lo = _split_bf16(w_ref[0])
    acc = jnp.dot(c_hi, w_hi, preferred_element_type=F32)
    acc += jnp.dot(c_lo, w_hi, preferred_element_type=F32)
    acc += jnp.dot(c_hi, w_lo, preferred_element_type=F32)
    o_ref[0] = acc + b_ref[0]


def _ada_call(c, w_ada, b_ada):
    depth, d, n = w_ada.shape
    b = c.shape[0]
    tn = n // 4
    return pl.pallas_call(
        _ada_kernel,
        grid=(depth, n // tn),
        in_specs=[
            pl.BlockSpec((b, d), lambda l, j: (0, 0)),
            pl.BlockSpec((1, d, tn), lambda l, j: (l, 0, j)),
            pl.BlockSpec((1, 1, tn), lambda l, j: (l, 0, j)),
        ],
        out_specs=pl.BlockSpec((1, b, tn), lambda l, j: (l, 0, j)),
        out_shape=jax.ShapeDtypeStruct((depth, b, n), F32),
        compiler_params=pltpu.CompilerParams(
            dimension_semantics=("arbitrary", "arbitrary"), vmem_limit_bytes=VMEM_LIMIT),
        name="ada_mod",
    )(c, w_ada, b_ada.reshape(depth, 1, n))


def _conformer_pieces(h_scr, ph_scr, cw_ref, cbias_ref, lng_ref, lnb_ref, dst_ref):
    rows, w = dst_ref.shape
    first = HALO - (CONF_KERNEL - 1)
    phases = []
    for phase in range(SUBLANES):
        taps = [k for k in range(CONF_KERNEL) if (first + k) % SUBLANES == phase]
        src = (h_scr, first + taps[0]) if phase == 0 else (ph_scr.at[phase], 0)
        phases.append((taps,) + src)

    def shifted_copies():
        for phase in range(1, SUBLANES):
            taps = phases[phase][0]
            base = first + taps[0]
            span = taps[-1] - taps[0] + rows
            ph_scr[phase, 0:span, :] = h_scr[base:base + span, :]

    def chunk(r, bias=None):
        acc = jnp.broadcast_to(cbias_ref[...] if bias is None else bias, (CONV_CHUNK, w))
        for taps, src, base in phases:
            window = src[r + base:r + base + taps[-1] - taps[0] + CONV_CHUNK, :]
            for k in taps:
                o = k - taps[0]
                acc = acc + cw_ref[k:k + 1, :] * window[o:o + CONV_CHUNK, :]
        mu = jnp.mean(acc, axis=-1, keepdims=True)
        cen = acc - mu
        var = jnp.mean(cen * cen, axis=-1, keepdims=True)
        y = cen * lax.rsqrt(var + EPS) * lng_ref[...] + lnb_ref[...]
        out = y * jax.nn.sigmoid(y)
        dst_ref[r:r + CONV_CHUNK, :] = out.astype(dst_ref.dtype)
        return out[0:ANCHOR_ROWS, 0:LANES]

    return [shifted_copies] + [functools.partial(chunk, r) for r in range(0, rows, CONV_CHUNK)]


def _inproj_mixers_kernel(x_ref, mod_ref, g_ref, w_ref, qkg_ref, ones32_ref, sw_ref, pw_ref, ps_ref,
                          qt_ref, k_ref, vt_ref, glu_ref, yo_ref,
                          hb_scr, pa_scr, pb_scr, c_scr, d_scr, e_scr, f_scr):
    ts = x_ref.shape[1]
    w = GROUP_WIDTH
    n_cols = w_ref.shape[2] // w
    rows = SUB_TILE
    j = pl.program_id(1)

    @pl.when(j == 0)
    def _():
        zeros = jnp.zeros((HALO, w), F32)
        c_scr[0:HALO, :] = zeros
        d_scr[0:HALO, :] = zeros

    lane = lax.broadcasted_iota(jnp.int32, (1, w), 1)
    group = lane // (w // len(POOL_WINDOWS))
    col = lambda c: slice(c * w, (c + 1) * w)

    def matmul_pieces(r0, dst):
        def normalise():
            hb_scr[...] = _modulated_norm(x_ref[0, r0:r0 + rows, :], g_ref[0], mod_ref[0, 0, 0:1, :],
                                          mod_ref[0, 0, 1:2, :]).astype(BF16)

        def project(c):
            dst[:, col(c)] = jnp.dot(hb_scr[...], w_ref[0, :, col(c)], preferred_element_type=F32)

        return [normalise] + [functools.partial(project, c) for c in range(n_cols)]

    def vector_pieces(r0, src):
        tile = slice(r0, r0 + rows)
        scr = slice(HALO + r0, HALO + r0 + rows)

        def qk_norm(c):
            p = src[:, col(c)]
            ms = _group_sum(p * p, ones32_ref[...]) * (1.0 / QK_DIM)
            p = p * lax.rsqrt(ms + EPS) * qkg_ref[0, :, col(c)]
            if c == 0:
                qt_ref[0, :, tile] = p.T.astype(qt_ref.dtype)
            else:
                k_ref[0, tile, :] = p.astype(k_ref.dtype)

        def v_and_glu():
            vt_ref[0, :, tile] = src[:, col(2)].T.astype(vt_ref.dtype)
            glu_ref[0, tile, :] = src[:, col(3)] * jax.nn.sigmoid(src[:, col(4)])

        def short_conv():
            c_scr[scr, :] = src[:, col(6)] * src[:, col(7)]
            conv = None
            for k in range(SHORT_KERNEL):
                off = HALO + r0 - (SHORT_KERNEL - 1) + k
                term = sw_ref[0, k:k + 1, :] * c_scr[off:off + rows, :]
                conv = term if conv is None else conv + term
            yo_ref[0, tile, col(0)] = (src[:, col(5)] * conv).astype(yo_ref.dtype)

        def pool():
            d_scr[scr, :] = src[:, col(8)]
            end = HALO + r0 + rows
            lo = r0 + SUBLANES
            e_scr[lo:end, :] = d_scr[lo:end, :] + d_scr[lo - 1:end - 1, :]
            pooled = e_scr[scr, :]
            win = jnp.full((1, w), POOL_WINDOWS[0], jnp.int32)
            a, b = e_scr, f_scr
            for g in range(1, len(POOL_WINDOWS)):
                shift = POOL_WINDOWS[g - 1]
                lo = r0 + SUBLANES * (g + 1)
                b[lo:end, :] = a[lo:end, :] + a[lo - shift:end - shift, :]
                pooled = jnp.where(group >= g, b[scr, :], pooled)
                win = jnp.where(group >= g, POOL_WINDOWS[g], win)
                a, b = b, a
            t_pos = j * ts + r0 + lax.broadcasted_iota(jnp.int32, (rows, w), 0)
            cnt = jnp.minimum(t_pos + 1, win).astype(F32)
            y = (pooled / cnt - d_scr[scr, :]).astype(BF16)
            yd = jnp.dot(y, pw_ref[0], preferred_element_type=F32) * ps_ref[0]
            yo_ref[0, tile, col(1)] = yd.astype(yo_ref.dtype)

        return [functools.partial(qk_norm, 0), functools.partial(qk_norm, 1), v_and_glu,
                short_conv, pool]

    buffers = (pa_scr, pb_scr)
    n_sub = ts // rows
    for i in range(n_sub + 1):
        mm = matmul_pieces(i * rows, buffers[i % 2]) if i < n_sub else []
        vec = vector_pieces((i - 1) * rows, buffers[(i - 1) % 2]) if i > 0 else []
        stride = max(1, len(mm) // max(1, len(vec)))
        for k in range(max(len(mm), stride * len(vec))):
            if k < len(mm):
                mm[k]()
            if k % stride == stride - 1 and k // stride < len(vec):
                vec[k // stride]()

    for buf in (c_scr, d_scr):
        buf[0:HALO, :] = buf[ts:ts + HALO, :]


def _inproj_mixers_call(layer, x, mods, norm_g, w_in, qk_gain, ones32, short_w, pool_bd, pool_scale):
    b, s, d = x.shape
    n = w_in.shape[2]
    ts = INPROJ_TILE
    w = GROUP_WIDTH
    per_layer = lambda *shape: pl.BlockSpec((1,) + shape, lambda i, j: (layer,) + (0,) * len(shape))
    rows_of = lambda width: pl.BlockSpec((1, ts, width), lambda i, j: (i, j, 0))
    transposed = pl.BlockSpec((1, w, ts), lambda i, j: (i, 0, j))
    return pl.pallas_call(
        _inproj_mixers_kernel,
        grid=(b, s // ts),
        in_specs=[
            rows_of(d),
            pl.BlockSpec((1, 1, N_MOD, d), lambda i, j: (layer, i, 0, 0)),
            per_layer(1, d),
            per_layer(d, n),
            per_layer(1, 2 * w),
            pl.BlockSpec((w, w), lambda i, j: (0, 0)),
            per_layer(SHORT_KERNEL, w), per_layer(w, w), per_layer(1, w),
        ],
        out_specs=[transposed, rows_of(w), transposed, rows_of(w), rows_of(2 * w)],
        out_shape=[jax.ShapeDtypeStruct((b, w, s), BF16),
                   jax.ShapeDtypeStruct((b, s, w), BF16),
                   jax.ShapeDtypeStruct((b, w, s), BF16),
                   jax.ShapeDtypeStruct((b, s, w), F32),
                   jax.ShapeDtypeStruct((b, s, 2 * w), BF16)],
        scratch_shapes=[pltpu.VMEM((SUB_TILE, d), BF16),
                        pltpu.VMEM((SUB_TILE, n), F32), pltpu.VMEM((SUB_TILE, n), F32)]
        + [pltpu.VMEM((HALO + ts, w), F32) for _ in range(4)],
        compiler_params=pltpu.CompilerParams(
            dimension_semantics=("arbitrary", "arbitrary"), vmem_limit_bytes=VMEM_LIMIT),
        name="inproj_mixers",
    )(x, mods, norm_g, w_in, qk_gain, ones32, short_w, pool_bd, pool_scale)


def _attn_kernel(q_ref, qn_ref, k_ref, vt_ref, lamp_ref, subg_ref, o_ref,
                 vt_scr, qmt_scr, qmn_scr, sa_scr, sb_scr, m_scr, acc_scr, ot_scr, *, lam_init):
    t = q_ref.shape[2]
    n_chunks = k_ref.shape[1] // t
    qi = pl.program_id(1)

    @pl.when(qi == 0)
    def _():
        for c in range(n_chunks):
            for head in range(ATTN_HEADS):
                r = head * VT_ROWS
                vt_scr[c, r:r + V_DIM, :] = vt_ref[0, head * V_DIM:(head + 1) * V_DIM, c * t:(c + 1) * t]
                vt_scr[c, r + V_DIM:r + VT_ROWS, :] = jnp.ones((VT_ROWS - V_DIM, t), BF16)
        qmt_scr[...] = jnp.zeros(qmt_scr.shape, BF16)
        qmn_scr[...] = jnp.zeros(qmn_scr.shape, BF16)

    for g in range(N_MAPS):
        rows = slice(g * QK_DIM, (g + 1) * QK_DIM)
        qmt_scr[g, rows, :] = q_ref[0, rows, :]
        qmn_scr[g, rows, :] = qn_ref[0, rows, :]
    m_scr[...] = jnp.full(m_scr.shape, -jnp.inf, F32)
    acc_scr[...] = jnp.zeros(acc_scr.shape, F32)

    def fold(x, op):
        parts = [x[r:r + SUBLANES, :] for r in range(0, x.shape[0], SUBLANES)]
        while len(parts) > 1:
            parts = [op(parts[i], parts[i + 1]) for i in range(0, len(parts), 2)]
        return parts[0]

    def stage(score_jobs, soft_chunk, soft_src, diagonal=False):
        keys = [k_ref[0, pl.ds(pl.multiple_of(job[0] * t, t), t), :] for job in score_jobs]
        if diagonal:
            key = lax.broadcasted_iota(jnp.int32, (t, t), 0)
            query = lax.broadcasted_iota(jnp.int32, (t, t), 1)
            keep = key <= query

        def scores(g):
            for kb, (_, qm, dst) in zip(keys, score_jobs):
                dst[g] = jnp.dot(kb, qm[g], preferred_element_type=F32)

        for g in range(SCORE_LEAD):
            scores(g)
        for g in range(N_MAPS):
            if g + SCORE_LEAD < N_MAPS:
                scores(g + SCORE_LEAD)
            if soft_chunk is None:
                continue
            head = g // 2
            st = soft_src[g]
            if diagonal:
                st = jnp.where(keep, st, -jnp.inf)
            m_prev = m_scr[g]
            m_new = jnp.maximum(m_prev, jnp.max(fold(st, jnp.maximum), axis=0, keepdims=True))
            alpha = jnp.exp2(m_prev - m_new)
            p = jnp.exp2((st - m_new).astype(BF16))
            m_scr[g] = m_new
            vt = vt_scr[soft_chunk, head * VT_ROWS:(head + 1) * VT_ROWS, :]
            pv = jnp.dot(vt, p, preferred_element_type=F32)
            acc = acc_scr[g] * alpha + pv
            if not diagonal:
                acc_scr[g] = acc
                continue
            o = acc[0:V_DIM, :] * (1.0 / acc[V_DIM:V_DIM + 1, :])
            if g % 2 == 0:
                o_first = o
            else:
                out = o_first - lam * o
                ms = jnp.mean(out * out, axis=0, keepdims=True)
                out = out * lax.rsqrt(ms + EPS) * subg_ref[0] * (1.0 - lam_init)
                ot_scr[head * V_DIM:(head + 1) * V_DIM, :] = out

    lp = lamp_ref[0]
    lam = (jnp.exp(jnp.sum(lp[0:1] * lp[1:2], axis=-1, keepdims=True))
           - jnp.exp(jnp.sum(lp[2:3] * lp[3:4], axis=-1, keepdims=True)) + lam_init)

    def run(first, second):
        @pl.when(qi == 0)
        def _():
            stage([(0, qmt_scr, first)], None, None)

        def body(i, carry):
            stage([(2 * i + 1, qmt_scr, second)], 2 * i, first)
            stage([(2 * i + 2, qmt_scr, first)], 2 * i + 1, second)
            return carry

        lax.fori_loop(0, qi // 2, body, 0)

        @pl.when(qi % 2 == 0)
        def _():
            stage([(0, qmn_scr, second)], qi, first, diagonal=True)

        @pl.when(qi % 2 == 1)
        def _():
            stage([(qi, qmt_scr, second)], qi - 1, first)
            stage([(0, qmn_scr, first)], qi, second, diagonal=True)

    arrives_in_b = ((qi + 1) // 2) % 2

    @pl.when(arrives_in_b == 0)
    def _():
        run(sa_scr, sb_scr)

    @pl.when(arrives_in_b == 1)
    def _():
        run(sb_scr, sa_scr)

    o_ref[0] = ot_scr[...].T.astype(o_ref.dtype)


def _attn_call(layer, qt, k, vt, lam_params, sub_gain, lam_init):
    b, s, _ = k.shape
    t = ATTN_TILE
    w = GROUP_WIDTH
    return pl.pallas_call(
        functools.partial(_attn_kernel, lam_init=lam_init),
        grid=(b, s // t),
        in_specs=[
            pl.BlockSpec((1, w, t), lambda i, j: (i, 0, j)),
            pl.BlockSpec((1, w, t), lambda i, j: (i, 0, jnp.minimum(j + 1, s // t - 1))),
            pl.BlockSpec((1, s, w), lambda i, j: (i, 0, 0)),
            pl.BlockSpec((1, w, s), lambda i, j: (i, 0, 0)),
            pl.BlockSpec((1, 4, QK_DIM), lambda i, j: (layer, 0, 0)),
            pl.BlockSpec((1, V_DIM, t), lambda i, j: (layer, 0, 0)),
        ],
        out_specs=pl.BlockSpec((1, t, w), lambda i, j: (i, j, 0)),
        out_shape=jax.ShapeDtypeStruct((b, s, w), BF16),
        scratch_shapes=[
            pltpu.VMEM((s // t, ATTN_HEADS * VT_ROWS, t), BF16),
            pltpu.VMEM((N_MAPS, w, t), BF16),
            pltpu.VMEM((N_MAPS, w, t), BF16),
            pltpu.VMEM((N_MAPS, t, t), F32),
            pltpu.VMEM((N_MAPS, t, t), F32),
            pltpu.VMEM((N_MAPS, 1, t), F32),
            pltpu.VMEM((N_MAPS, VT_ROWS, t), F32),
            pltpu.VMEM((w, t), F32),
        ],
        compiler_params=pltpu.CompilerParams(
            dimension_semantics=("arbitrary", "arbitrary"), vmem_limit_bytes=VMEM_LIMIT),
        name="diff_attn",
    )(qt, qt, k, vt, lam_params, sub_gain)


def _outproj_mlp_kernel(x_ref, ya_ref, yo_ref, glu0_ref, glun_ref, mod_ref, g_ref,
                        wout_ref, w1_ref, w2_ref, cw_ref, cbias_ref, lng_ref, lnb_ref,
                        o_ref, x1_scr, hb_scr, a_scr, yb_scr, hw_scr, ph_scr):
    tm = x_ref.shape[1]
    w = GROUP_WIDTH
    d_ff = w1_ref.shape[2]
    j = pl.program_id(1)
    n_tiles = pl.num_programs(1)
    first_step = jnp.logical_and(pl.program_id(0) == 0, j == 0)
    mod = lambda r: mod_ref[0, 0, r:r + 1, :]

    def conformer_pieces():
        return _conformer_pieces(hw_scr, ph_scr, cw_ref.at[0], cbias_ref.at[0], lng_ref.at[0],
                                 lnb_ref.at[0], yb_scr)

    @pl.when(first_step)
    def _():
        hw_scr[0:HALO, :] = jnp.zeros((HALO, w), F32)
        hw_scr[HALO:, :] = glu0_ref[0]
        for piece in conformer_pieces():
            piece()
        hw_scr[0:HALO, :] = hw_scr[tm:, :]

    mix = jnp.dot(ya_ref[0], wout_ref[0, 0:w, :], preferred_element_type=F32)
    mix = mix + jnp.dot(yb_scr[...], wout_ref[0, w:2 * w, :], preferred_element_type=F32)
    mix = mix + jnp.dot(yo_ref[0], wout_ref[0, 2 * w:, :], preferred_element_type=F32)
    x1 = x_ref[0] + mod(2) * mix
    x1_scr[...] = x1
    hb_scr[...] = _modulated_norm(x1, g_ref[0], mod(3), mod(4)).astype(BF16)

    tail = hw_scr[0:HALO, :]
    hw_scr[0:HALO, :] = jnp.where(j == n_tiles - 1, 0.0, tail)
    hw_scr[HALO:, :] = glun_ref[0]
    pieces = conformer_pieces()
    pieces.pop(0)()

    always = j < n_tiles

    def anchor(ref, cols, slab):
        kept = ref[0:ANCHOR_ROWS, cols]
        ref[0:ANCHOR_ROWS, cols] = jnp.where(always, kept, slab.astype(kept.dtype))

    chunk = 1024
    n_chunks = d_ff // chunk
    per_chunk = len(pieces) // (2 * n_chunks)
    first_lanes = slice(0, LANES)

    def mixer_group(released_by, next_input, next_cols):
        bias = jnp.where(always, cbias_ref[0], released_by[0:1, 0:w])
        for _ in range(per_chunk):
            anchor(next_input, next_cols, pieces.pop(0)(bias))

    released_by = x1
    for ci in range(n_chunks):
        c = ci * chunk
        a = jnp.dot(hb_scr[...], w1_ref[0, :, c:c + chunk], preferred_element_type=F32)
        a_scr[:, c:c + chunk] = jnp.square(jnp.maximum(a, 0.0)).astype(BF16)
        if ci + 1 < n_chunks:
            mixer_group(released_by, hb_scr, first_lanes)
        else:
            mixer_group(released_by, a_scr, first_lanes)
        released_by = a
    y = None
    for ci in range(n_chunks):
        c = ci * chunk
        part = jnp.dot(a_scr[:, c:c + chunk], w2_ref[0, c:c + chunk, :], preferred_element_type=F32)
        y = part if y is None else y + part
        if ci + 1 < n_chunks:
            mixer_group(released_by, a_scr, slice(c + chunk, c + chunk + LANES))
        else:
            mixer_group(released_by, x1_scr, first_lanes)
        released_by = part
    for piece in pieces:
        piece()
    hw_scr[0:HALO, :] = hw_scr[tm:, :]
    o_ref[0] = x1_scr[...] + mod(5) * y


def _outproj_mlp_call(layer, x, ya, glu, yo, mods, norm_g, w_out, w1, w2, conf_w, conf_b, ln_g, ln_b):
    b, s, d = x.shape
    d_ff = w1.shape[2]
    tm = ROW_TILE
    w = GROUP_WIDTH
    n_tiles = s // tm
    resident = lambda arr: pl.BlockSpec((1,) + arr.shape[1:], lambda i, j: (layer, 0, 0),
                                        pipeline_mode=pl.Buffered(1))
    per_layer = lambda *shape: pl.BlockSpec((1,) + shape, lambda i, j: (layer,) + (0,) * len(shape))
    rows_of = lambda width: pl.BlockSpec((1, tm, width), lambda i, j: (i, j, 0))

    def next_tile(i, j):
        t = jnp.minimum(i * n_tiles + j + 1, b * n_tiles - 1)
        return (t // n_tiles, t % n_tiles, 0)

    return pl.pallas_call(
        _outproj_mlp_kernel,
        grid=(b, n_tiles),
        in_specs=[
            rows_of(d), rows_of(ya.shape[2]), rows_of(yo.shape[2]),
            pl.BlockSpec((1, tm, w), lambda i, j: (0, 0, 0)),
            pl.BlockSpec((1, tm, w), next_tile),
            pl.BlockSpec((1, 1, N_MOD, d), lambda i, j: (layer, i, 0, 0)),
            per_layer(1, d),
            resident(w_out), resident(w1), resident(w2),
            per_layer(CONF_KERNEL, w), per_layer(1, w), per_layer(1, w), per_layer(1, w),
        ],
        out_specs=rows_of(d),
        out_shape=jax.ShapeDtypeStruct((b, s, d), F32),
        scratch_shapes=[
            pltpu.VMEM((tm, d), F32),
            pltpu.VMEM((tm, d), BF16),
            pltpu.VMEM((tm, d_ff), BF16),
            pltpu.VMEM((tm, w), BF16),
            pltpu.VMEM((HALO + tm, w), F32),
            pltpu.VMEM((SUBLANES, HALO + tm, w), F32),
        ],
        compiler_params=pltpu.CompilerParams(
            dimension_semantics=("arbitrary", "arbitrary"), vmem_limit_bytes=VMEM_LIMIT),
        name="outproj_mlp",
    )(x, ya, yo, glu, glu, mods, norm_g, w_out, w1, w2, conf_w, conf_b, ln_g, ln_b)


def _block_diag_ones(width, group):
    idx = jnp.arange(width) // group
    return (idx[:, None] == idx[None, :]).astype(BF16)


def kernel(x, c, w_ada, b_ada, norm1_g, norm2_g, w_in, w_out, q_norm_g, k_norm_g, lam_params,
           attn_sub_g, conf_dw_w, conf_dw_b, conf_ln_g, conf_ln_b, short_conv_w, pool_w, pool_scale,
           w_ff1, w_ff2):
    depth, d, _ = w_in.shape
    b = x.shape[0]
    w = GROUP_WIDTH
    mods = _ada_call(c, w_ada, b_ada).reshape(depth, b, N_MOD, d)
    ones32 = _block_diag_ones(w, QK_DIM)
    rows = lambda v: v.reshape(depth, 1, -1)

    qk_gain = jnp.concatenate([jnp.tile(q_norm_g, (1, N_MAPS)) * (QK_DIM ** -0.5 * LOG2_E),
                               jnp.tile(k_norm_g, (1, N_MAPS))], axis=1).reshape(depth, 1, 2 * w)
    sub_gain = jnp.broadcast_to(attn_sub_g[:, :, None], (depth, V_DIM, ATTN_TILE))
    n_pool = len(POOL_WINDOWS)
    group_eye = jnp.eye(n_pool, dtype=pool_w.dtype)
    pool_bd = jnp.einsum('lgcd,gh->lgchd', pool_w, group_eye).reshape(depth, w, w).astype(BF16)
    w_in_b, w_out_b = w_in.astype(BF16), w_out.astype(BF16)
    w_ff1_b, w_ff2_b = w_ff1.astype(BF16), w_ff2.astype(BF16)

    for l in range(depth):
        lam_init = 0.8 - 0.6 * math.exp(-0.3 * l)
        qt, k, vt, glu, yo = _inproj_mixers_call(
            l, x, mods, rows(norm1_g), w_in_b, qk_gain, ones32,
            short_conv_w, pool_bd, rows(pool_scale))
        ya = _attn_call(l, qt, k, vt, lam_params, sub_gain, lam_init)
        x = _outproj_mlp_call(l, x, ya, glu, yo, mods, rows(norm2_g), w_out_b, w_ff1_b, w_ff2_b,
                              conf_dw_w, rows(conf_dw_b), rows(conf_ln_g), rows(conf_ln_b))
    return x
```

```python
import functools
import math

import jax
import jax.numpy as jnp
from jax import lax
from jax.experimental import pallas as pl
from jax.experimental.pallas import tpu as pltpu

F32 = jnp.float32
BF16 = jnp.bfloat16

EPS = 1e-6
LOG2_E = math.log2(math.e)
N_MOD = 6
ATTN_HEADS = 4
QK_DIM = 32
V_DIM = 64
VT_ROWS = V_DIM + 16
N_MAPS = 2 * ATTN_HEADS
CONF_KERNEL = 31
SHORT_KERNEL = 3
POOL_WINDOWS = (2, 4, 8, 16)
GROUP_WIDTH = 256
SUBLANES = 8
LANES = 128
ANCHOR_ROWS = 16

ROW_TILE = 512
INPROJ_TILE = 1024
SUB_TILE = 256
ATTN_TILE = 256
SCORE_LEAD = 2
CONV_CHUNK = 64
HALO = 32
VMEM_LIMIT = 56 * 1024 * 1024


def _split_bf16(x):
    hi = x.astype(BF16)
    lo = (x - hi.astype(F32)).astype(BF16)
    return hi, lo


def _group_sum(x, ones_blockdiag):
    return jnp.dot(x.astype(BF16), ones_blockdiag, preferred_element_type=F32)


def _modulated_norm(x, gain, shift, scale):
    ms = jnp.mean(x * x, axis=-1, keepdims=True)
    return (x * lax.rsqrt(ms + EPS)) * (gain * (1.0 + scale)) + shift


def _ada_kernel(c_ref, w_ref, b_ref, o_ref):
    c = c_ref[...]
    c_act = c * jax.nn.sigmoid(c)
    c_hi, c_lo = _split_bf16(c_act)
    w_hi, w_lo = _split_bf16(w_ref[0])
    acc = jnp.dot(c_hi, w_hi, preferred_element_type=F32)
    acc += jnp.dot(c_lo, w_hi, preferred_element_type=F32)
    acc += jnp.dot(c_hi, w_lo, preferred_element_type=F32)
    o_ref[0] = acc + b_ref[0]


def _ada_call(c, w_ada, b_ada):
    depth, d, n = w_ada.shape
    b = c.shape[0]
    tn = n // 4
    return pl.pallas_call(
        _ada_kernel,
        grid=(depth, n // tn),
        in_specs=[
            pl.BlockSpec((b, d), lambda l, j: (0, 0)),
            pl.BlockSpec((1, d, tn), lambda l, j: (l, 0, j)),
            pl.BlockSpec((1, 1, tn), lambda l, j: (l, 0, j)),
        ],
        out_specs=pl.BlockSpec((1, b, tn), lambda l, j: (l, 0, j)),
        out_shape=jax.ShapeDtypeStruct((depth, b, n), F32),
        compiler_params=pltpu.CompilerParams(
            dimension_semantics=("arbitrary", "arbitrary"), vmem_limit_bytes=VMEM_LIMIT),
        name="ada_mod",
    )(c, w_ada, b_ada.reshape(depth, 1, n))


def _conformer_pieces(h_scr, ph_scr, cw_ref, cbias_ref, lng_ref, lnb_ref, dst_ref):
    rows, w = dst_ref.shape
    first = HALO - (CONF_KERNEL - 1)
    phases = []
    for phase in range(SUBLANES):
        taps = [k for k in range(CONF_KERNEL) if (first + k) % SUBLANES == phase]
        src = (h_scr, first + taps[0]) if phase == 0 else (ph_scr.at[phase], 0)
        phases.append((taps,) + src)

    def shifted_copies():
        for phase in range(1, SUBLANES):
            taps = phases[phase][0]
            base = first + taps[0]
            span = taps[-1] - taps[0] + rows
            ph_scr[phase, 0:span, :] = h_scr[base:base + span, :]

    def chunk(r, bias=None):
        acc = jnp.broadcast_to(cbias_ref[...] if bias is None else bias, (CONV_CHUNK, w))
        for taps, src, base in phases:
            window = src[r + base:r + base + taps[-1] - taps[0] + CONV_CHUNK, :]
            for k in taps:
                o = k - taps[0]
                acc = acc + cw_ref[k:k + 1, :] * window[o:o + CONV_CHUNK, :]
        mu = jnp.mean(acc, axis=-1, keepdims=True)
        cen = acc - mu
        var = jnp.mean(cen * cen, axis=-1, keepdims=True)
        y = cen * lax.rsqrt(var + EPS) * lng_ref[...] + lnb_ref[...]
        out = y * jax.nn.sigmoid(y)
        dst_ref[r:r + CONV_CHUNK, :] = out.astype(dst_ref.dtype)
        return out[0:ANCHOR_ROWS, 0:LANES]

    return [shifted_copies] + [functools.partial(chunk, r) for r in range(0, rows, CONV_CHUNK)]


def _inproj_mixers_kernel(*refs, n_slabs):
    (x_ref, mod_ref, g_ref, w_ref, qkg_ref, ones32_ref, sw_ref, pw_ref, ps_ref) = refs[:9]
    wide_refs = refs[9:9 + n_slabs]
    qt_ref, k_ref, vt_ref, glu_ref, yo_ref = refs[9 + n_slabs:14 + n_slabs]
    narrow_refs = refs[14 + n_slabs:14 + 2 * n_slabs]
    hb_scr, pa_scr, pb_scr, c_scr, d_scr, e_scr, f_scr = refs[14 + 2 * n_slabs:]

    for wide, narrow in zip(wide_refs, narrow_refs):
        narrow[...] = wide[0].astype(narrow.dtype)

    ts = x_ref.shape[1]
    w = GROUP_WIDTH
    n_cols = w_ref.shape[1] // w
    rows = SUB_TILE
    j = pl.program_id(1)

    @pl.when(j == 0)
    def _():
        zeros = jnp.zeros((HALO, w), F32)
        c_scr[0:HALO, :] = zeros
        d_scr[0:HALO, :] = zeros

    lane = lax.broadcasted_iota(jnp.int32, (1, w), 1)
    group = lane // (w // len(POOL_WINDOWS))
    col = lambda c: slice(c * w, (c + 1) * w)

    def matmul_pieces(r0, dst):
        def normalise():
            hb_scr[...] = _modulated_norm(x_ref[0, r0:r0 + rows, :], g_ref[0], mod_ref[0, 0, 0:1, :],
                                          mod_ref[0, 0, 1:2, :]).astype(BF16)

        def project(c):
            dst[:, col(c)] = jnp.dot(hb_scr[...], w_ref[:, col(c)], preferred_element_type=F32)

        return [normalise] + [functools.partial(project, c) for c in range(n_cols)]

    def vector_pieces(r0, src):
        tile = slice(r0, r0 + rows)
        scr = slice(HALO + r0, HALO + r0 + rows)

        def qk_norm(c):
            p = src[:, col(c)]
            ms = _group_sum(p * p, ones32_ref[...]) * (1.0 / QK_DIM)
            p = p * lax.rsqrt(ms + EPS) * qkg_ref[0, :, col(c)]
            if c == 0:
                qt_ref[0, :, tile] = p.T.astype(qt_ref.dtype)
            else:
                k_ref[0, tile, :] = p.astype(k_ref.dtype)

        def v_and_glu():
            vt_ref[0, :, tile] = src[:, col(2)].T.astype(vt_ref.dtype)
            glu_ref[0, tile, :] = src[:, col(3)] * jax.nn.sigmoid(src[:, col(4)])

        def short_conv():
            c_scr[scr, :] = src[:, col(6)] * src[:, col(7)]
            conv = None
            for k in range(SHORT_KERNEL):
                off = HALO + r0 - (SHORT_KERNEL - 1) + k
                term = sw_ref[0, k:k + 1, :] * c_scr[off:off + rows, :]
                conv = term if conv is None else conv + term
            yo_ref[0, tile, col(0)] = (src[:, col(5)] * conv).astype(yo_ref.dtype)

        def pool():
            d_scr[scr, :] = src[:, col(8)]
            end = HALO + r0 + rows
            lo = r0 + SUBLANES
            e_scr[lo:end, :] = d_scr[lo:end, :] + d_scr[lo - 1:end - 1, :]
            pooled = e_scr[scr, :]
            win = jnp.full((1, w), POOL_WINDOWS[0], jnp.int32)
            a, b = e_scr, f_scr
            for g in range(1, len(POOL_WINDOWS)):
                shift = POOL_WINDOWS[g - 1]
                lo = r0 + SUBLANES * (g + 1)
                b[lo:end, :] = a[lo:end, :] + a[lo - shift:end - shift, :]
                pooled = jnp.where(group >= g, b[scr, :], pooled)
                win = jnp.where(group >= g, POOL_WINDOWS[g], win)
                a, b = b, a
            t_pos = j * ts + r0 + lax.broadcasted_iota(jnp.int32, (rows, w), 0)
            cnt = jnp.minimum(t_pos + 1, win).astype(F32)
            y = (pooled / cnt - d_scr[scr, :]).astype(BF16)
            yd = jnp.dot(y, pw_ref[0], preferred_element_type=F32) * ps_ref[0]
            yo_ref[0, tile, col(1)] = yd.astype(yo_ref.dtype)

        return [functools.partial(qk_norm, 0), functools.partial(qk_norm, 1), v_and_glu,
                short_conv, pool]

    buffers = (pa_scr, pb_scr)
    n_sub = ts // rows
    for i in range(n_sub + 1):
        mm = matmul_pieces(i * rows, buffers[i % 2]) if i < n_sub else []
        vec = vector_pieces((i - 1) * rows, buffers[(i - 1) % 2]) if i > 0 else []
        stride = max(1, len(mm) // max(1, len(vec)))
        for k in range(max(len(mm), stride * len(vec))):
            if k < len(mm):
                mm[k]()
            if k % stride == stride - 1 and k // stride < len(vec):
                vec[k // stride]()

    for buf in (c_scr, d_scr):
        buf[0:HALO, :] = buf[ts:ts + HALO, :]


def _inproj_mixers_call(layer, x, mods, norm_g, w_in, qk_gain, ones32, short_w, pool_bd, pool_scale,
                        later_weights):
    b, s, d = x.shape
    n = w_in.shape[1]
    ts = INPROJ_TILE
    w = GROUP_WIDTH
    n_tiles = s // ts
    n_steps = b * n_tiles
    per_layer = lambda *shape: pl.BlockSpec((1,) + shape, lambda i, j: (layer,) + (0,) * len(shape))
    rows_of = lambda width: pl.BlockSpec((1, ts, width), lambda i, j: (i, j, 0))
    transposed = pl.BlockSpec((1, w, ts), lambda i, j: (i, 0, j))
    slab_in, slab_out, slab_shape = [], [], []
    for wide, wide_layer in later_weights:
        _, r, c = wide.shape
        slab_in.append(pl.BlockSpec((1, r // n_steps, c),
                                    functools.partial(lambda wl, i, j: (wl, i * n_tiles + j, 0), wide_layer)))
        slab_out.append(pl.BlockSpec((r // n_steps, c), lambda i, j: (i * n_tiles + j, 0)))
        slab_shape.append(jax.ShapeDtypeStruct((r, c), BF16))
    return pl.pallas_call(
        functools.partial(_inproj_mixers_kernel, n_slabs=len(later_weights)),
        grid=(b, s // ts),
        in_specs=[
            rows_of(d),
            pl.BlockSpec((1, 1, N_MOD, d), lambda i, j: (layer, i, 0, 0)),
            per_layer(1, d),
            pl.BlockSpec((d, n), lambda i, j: (0, 0)),
            per_layer(1, 2 * w),
            pl.BlockSpec((w, w), lambda i, j: (0, 0)),
            per_layer(SHORT_KERNEL, w), per_layer(w, w), per_layer(1, w),
        ] + slab_in,
        out_specs=[transposed, rows_of(w), transposed, rows_of(w), rows_of(2 * w)] + slab_out,
        out_shape=[jax.ShapeDtypeStruct((b, w, s), BF16),
                   jax.ShapeDtypeStruct((b, s, w), BF16),
                   jax.ShapeDtypeStruct((b, w, s), BF16),
                   jax.ShapeDtypeStruct((b, s, w), F32),
                   jax.ShapeDtypeStruct((b, s, 2 * w), BF16)] + slab_shape,
        scratch_shapes=[pltpu.VMEM((SUB_TILE, d), BF16),
                        pltpu.VMEM((SUB_TILE, n), F32), pltpu.VMEM((SUB_TILE, n), F32)]
        + [pltpu.VMEM((HALO + ts, w), F32) for _ in range(4)],
        compiler_params=pltpu.CompilerParams(
            dimension_semantics=("arbitrary", "arbitrary"), vmem_limit_bytes=VMEM_LIMIT),
        name="inproj_mixers",
    )(x, mods, norm_g, w_in, qk_gain, ones32, short_w, pool_bd, pool_scale,
      *[wide for wide, _ in later_weights])


def _attn_kernel(q_ref, qn_ref, k_ref, vt_ref, lamp_ref, subg_ref, o_ref,
                 vt_scr, qmt_scr, qmn_scr, sa_scr, sb_scr, m_scr, acc_scr, ot_scr, *, lam_init):
    t = q_ref.shape[2]
    n_chunks = k_ref.shape[1] // t
    qi = pl.program_id(1)

    @pl.when(qi == 0)
    def _():
        for c in range(n_chunks):
            for head in range(ATTN_HEADS):
                r = head * VT_ROWS
                vt_scr[c, r:r + V_DIM, :] = vt_ref[0, head * V_DIM:(head + 1) * V_DIM, c * t:(c + 1) * t]
                vt_scr[c, r + V_DIM:r + VT_ROWS, :] = jnp.ones((VT_ROWS - V_DIM, t), BF16)
        qmt_scr[...] = jnp.zeros(qmt_scr.shape, BF16)
        qmn_scr[...] = jnp.zeros(qmn_scr.shape, BF16)

    for g in range(N_MAPS):
        rows = slice(g * QK_DIM, (g + 1) * QK_DIM)
        qmt_scr[g, rows, :] = q_ref[0, rows, :]
        qmn_scr[g, rows, :] = qn_ref[0, rows, :]
    m_scr[...] = jnp.full(m_scr.shape, -jnp.inf, F32)
    acc_scr[...] = jnp.zeros(acc_scr.shape, F32)

    def fold(x, op):
        parts = [x[r:r + SUBLANES, :] for r in range(0, x.shape[0], SUBLANES)]
        while len(parts) > 1:
            parts = [op(parts[i], parts[i + 1]) for i in range(0, len(parts), 2)]
        return parts[0]

    def stage(score_jobs, soft_chunk, soft_src, diagonal=False):
        keys = [k_ref[0, pl.ds(pl.multiple_of(job[0] * t, t), t), :] for job in score_jobs]
        if diagonal:
            key = lax.broadcasted_iota(jnp.int32, (t, t), 0)
            query = lax.broadcasted_iota(jnp.int32, (t, t), 1)
            keep = key <= query

        def scores(g):
            for kb, (_, qm, dst) in zip(keys, score_jobs):
                dst[g] = jnp.dot(kb, qm[g], preferred_element_type=F32)

        for g in range(SCORE_LEAD):
            scores(g)
        for g in range(N_MAPS):
            if g + SCORE_LEAD < N_MAPS:
                scores(g + SCORE_LEAD)
            if soft_chunk is None:
                continue
            head = g // 2
            st = soft_src[g]
            if diagonal:
                st = jnp.where(keep, st, -jnp.inf)
            m_prev = m_scr[g]
            m_new = jnp.maximum(m_prev, jnp.max(fold(st, jnp.maximum), axis=0, keepdims=True))
            alpha = jnp.exp2(m_prev - m_new)
            p = jnp.exp2((st - m_new).astype(BF16))
            m_scr[g] = m_new
            vt = vt_scr[soft_chunk, head * VT_ROWS:(head + 1) * VT_ROWS, :]
            pv = jnp.dot(vt, p, preferred_element_type=F32)
            acc = acc_scr[g] * alpha + pv
            if not diagonal:
                acc_scr[g] = acc
                continue
            o = acc[0:V_DIM, :] * (1.0 / acc[V_DIM:V_DIM + 1, :])
            if g % 2 == 0:
                o_first = o
            else:
                out = o_first - lam * o
                ms = jnp.mean(out * out, axis=0, keepdims=True)
                out = out * lax.rsqrt(ms + EPS) * subg_ref[0] * (1.0 - lam_init)
                ot_scr[head * V_DIM:(head + 1) * V_DIM, :] = out

    lp = lamp_ref[0]
    lam = (jnp.exp(jnp.sum(lp[0:1] * lp[1:2], axis=-1, keepdims=True))
           - jnp.exp(jnp.sum(lp[2:3] * lp[3:4], axis=-1, keepdims=True)) + lam_init)

    def run(first, second):
        @pl.when(qi == 0)
        def _():
            stage([(0, qmt_scr, first)], None, None)

        def body(i, carry):
            stage([(2 * i + 1, qmt_scr, second)], 2 * i, first)
            stage([(2 * i + 2, qmt_scr, first)], 2 * i + 1, second)
            return carry

        lax.fori_loop(0, qi // 2, body, 0)

        @pl.when(qi % 2 == 0)
        def _():
            stage([(0, qmn_scr, second)], qi, first, diagonal=True)

        @pl.when(qi % 2 == 1)
        def _():
            stage([(qi, qmt_scr, second)], qi - 1, first)
            stage([(0, qmn_scr, first)], qi, second, diagonal=True)

    arrives_in_b = ((qi + 1) // 2) % 2

    @pl.when(arrives_in_b == 0)
    def _():
        run(sa_scr, sb_scr)

    @pl.when(arrives_in_b == 1)
    def _():
        run(sb_scr, sa_scr)

    o_ref[0] = ot_scr[...].T.astype(o_ref.dtype)


def _attn_call(layer, qt, k, vt, lam_params, sub_gain, lam_init):
    b, s, _ = k.shape
    t = ATTN_TILE
    w = GROUP_WIDTH
    return pl.pallas_call(
        functools.partial(_attn_kernel, lam_init=lam_init),
        grid=(b, s // t),
        in_specs=[
            pl.BlockSpec((1, w, t), lambda i, j: (i, 0, j)),
            pl.BlockSpec((1, w, t), lambda i, j: (i, 0, jnp.minimum(j + 1, s // t - 1))),
            pl.BlockSpec((1, s, w), lambda i, j: (i, 0, 0)),
            pl.BlockSpec((1, w, s), lambda i, j: (i, 0, 0)),
            pl.BlockSpec((1, 4, QK_DIM), lambda i, j: (layer, 0, 0)),
            pl.BlockSpec((1, V_DIM, t), lambda i, j: (layer, 0, 0)),
        ],
        out_specs=pl.BlockSpec((1, t, w), lambda i, j: (i, j, 0)),
        out_shape=jax.ShapeDtypeStruct((b, s, w), BF16),
        scratch_shapes=[
            pltpu.VMEM((s // t, ATTN_HEADS * VT_ROWS, t), BF16),
            pltpu.VMEM((N_MAPS, w, t), BF16),
            pltpu.VMEM((N_MAPS, w, t), BF16),
            pltpu.VMEM((N_MAPS, t, t), F32),
            pltpu.VMEM((N_MAPS, t, t), F32),
            pltpu.VMEM((N_MAPS, 1, t), F32),
            pltpu.VMEM((N_MAPS, VT_ROWS, t), F32),
            pltpu.VMEM((w, t), F32),
        ],
        compiler_params=pltpu.CompilerParams(
            dimension_semantics=("arbitrary", "arbitrary"), vmem_limit_bytes=VMEM_LIMIT),
        name="diff_attn",
    )(qt, qt, k, vt, lam_params, sub_gain)


def _outproj_mlp_kernel(x_ref, ya_ref, yo_ref, glu0_ref, glun_ref, mod_ref, g_ref,
                        wout_ref, w1_ref, w2_ref, cw_ref, cbias_ref, lng_ref, lnb_ref,
                        o_ref, x1_scr, hb_scr, a_scr, yb_scr, hw_scr, ph_scr):
    tm = x_ref.shape[1]
    w = GROUP_WIDTH
    d_ff = w1_ref.shape[1]
    j = pl.program_id(1)
    n_tiles = pl.num_programs(1)
    first_step = jnp.logical_and(pl.program_id(0) == 0, j == 0)
    mod = lambda r: mod_ref[0, 0, r:r + 1, :]

    def conformer_pieces():
        return _conformer_pieces(hw_scr, ph_scr, cw_ref.at[0], cbias_ref.at[0], lng_ref.at[0],
                                 lnb_ref.at[0], yb_scr)

    @pl.when(first_step)
    def _():
        hw_scr[0:HALO, :] = jnp.zeros((HALO, w), F32)
        hw_scr[HALO:, :] = glu0_ref[0]
        for piece in conformer_pieces():
            piece()
        hw_scr[0:HALO, :] = hw_scr[tm:, :]

    mix = jnp.dot(ya_ref[0], wout_ref[0:w, :], preferred_element_type=F32)
    mix = mix + jnp.dot(yb_scr[...], wout_ref[w:2 * w, :], preferred_element_type=F32)
    mix = mix + jnp.dot(yo_ref[0], wout_ref[2 * w:, :], preferred_element_type=F32)
    x1 = x_ref[0] + mod(2) * mix
    x1_scr[...] = x1
    hb_scr[...] = _modulated_norm(x1, g_ref[0], mod(3), mod(4)).astype(BF16)

    tail = hw_scr[0:HALO, :]
    hw_scr[0:HALO, :] = jnp.where(j == n_tiles - 1, 0.0, tail)
    hw_scr[HALO:, :] = glun_ref[0]
    pieces = conformer_pieces()
    pieces.pop(0)()

    always = j < n_tiles

    def anchor(ref, cols, slab):
        kept = ref[0:ANCHOR_ROWS, cols]
        ref[0:ANCHOR_ROWS, cols] = jnp.where(always, kept, slab.astype(kept.dtype))

    chunk = 1024
    n_chunks = d_ff // chunk
    per_chunk = len(pieces) // (2 * n_chunks)
    first_lanes = slice(0, LANES)

    def mixer_group(released_by, next_input, next_cols):
        bias = jnp.where(always, cbias_ref[0], released_by[0:1, 0:w])
        for _ in range(per_chunk):
            anchor(next_input, next_cols, pieces.pop(0)(bias))

    released_by = x1
    for ci in range(n_chunks):
        c = ci * chunk
        a = jnp.dot(hb_scr[...], w1_ref[:, c:c + chunk], preferred_element_type=F32)
        a_scr[:, c:c + chunk] = jnp.square(jnp.maximum(a, 0.0)).astype(BF16)
        if ci + 1 < n_chunks:
            mixer_group(released_by, hb_scr, first_lanes)
        else:
            mixer_group(released_by, a_scr, first_lanes)
        released_by = a
    y = None
    for ci in range(n_chunks):
        c = ci * chunk
        part = jnp.dot(a_scr[:, c:c + chunk], w2_ref[c:c + chunk, :], preferred_element_type=F32)
        y = part if y is None else y + part
        if ci + 1 < n_chunks:
            mixer_group(released_by, a_scr, slice(c + chunk, c + chunk + LANES))
        else:
            mixer_group(released_by, x1_scr, first_lanes)
        released_by = part
    for piece in pieces:
        piece()
    hw_scr[0:HALO, :] = hw_scr[tm:, :]
    o_ref[0] = x1_scr[...] + mod(5) * y


def _outproj_mlp_call(layer, x, ya, glu, yo, mods, norm_g, w_out, w1, w2, conf_w, conf_b, ln_g, ln_b):
    b, s, d = x.shape
    d_ff = w1.shape[1]
    tm = ROW_TILE
    w = GROUP_WIDTH
    n_tiles = s // tm
    resident = lambda arr: pl.BlockSpec(arr.shape, lambda i, j: (0, 0), pipeline_mode=pl.Buffered(1))
    per_layer = lambda *shape: pl.BlockSpec((1,) + shape, lambda i, j: (layer,) + (0,) * len(shape))
    rows_of = lambda width: pl.BlockSpec((1, tm, width), lambda i, j: (i, j, 0))

    def next_tile(i, j):
        t = jnp.minimum(i * n_tiles + j + 1, b * n_tiles - 1)
        return (t // n_tiles, t % n_tiles, 0)

    return pl.pallas_call(
        _outproj_mlp_kernel,
        grid=(b, n_tiles),
        in_specs=[
            rows_of(d), rows_of(ya.shape[2]), rows_of(yo.shape[2]),
            pl.BlockSpec((1, tm, w), lambda i, j: (0, 0, 0)),
            pl.BlockSpec((1, tm, w), next_tile),
            pl.BlockSpec((1, 1, N_MOD, d), lambda i, j: (layer, i, 0, 0)),
            per_layer(1, d),
            resident(w_out), resident(w1), resident(w2),
            per_layer(CONF_KERNEL, w), per_layer(1, w), per_layer(1, w), per_layer(1, w),
        ],
        out_specs=rows_of(d),
        out_shape=jax.ShapeDtypeStruct((b, s, d), F32),
        scratch_shapes=[
            pltpu.VMEM((tm, d), F32),
            pltpu.VMEM((tm, d), BF16),
            pltpu.VMEM((tm, d_ff), BF16),
            pltpu.VMEM((tm, w), BF16),
            pltpu.VMEM((HALO + tm, w), F32),
            pltpu.VMEM((SUBLANES, HALO + tm, w), F32),
        ],
        compiler_params=pltpu.CompilerParams(
            dimension_semantics=("arbitrary", "arbitrary"), vmem_limit_bytes=VMEM_LIMIT),
        name="outproj_mlp",
    )(x, ya, yo, glu, glu, mods, norm_g, w_out, w1, w2, conf_w, conf_b, ln_g, ln_b)


def _block_diag_ones(width, group):
    idx = jnp.arange(width) // group
    return (idx[:, None] == idx[None, :]).astype(BF16)


def kernel(x, c, w_ada, b_ada, norm1_g, norm2_g, w_in, w_out, q_norm_g, k_norm_g, lam_params,
           attn_sub_g, conf_dw_w, conf_dw_b, conf_ln_g, conf_ln_b, short_conv_w, pool_w, pool_scale,
           w_ff1, w_ff2):
    depth, d, _ = w_in.shape
    b = x.shape[0]
    w = GROUP_WIDTH
    mods = _ada_call(c, w_ada, b_ada).reshape(depth, b, N_MOD, d)
    ones32 = _block_diag_ones(w, QK_DIM)
    rows = lambda v: v.reshape(depth, 1, -1)

    qk_gain = jnp.concatenate([jnp.tile(q_norm_g, (1, N_MAPS)) * (QK_DIM ** -0.5 * LOG2_E),
                               jnp.tile(k_norm_g, (1, N_MAPS))], axis=1).reshape(depth, 1, 2 * w)
    sub_gain = jnp.broadcast_to(attn_sub_g[:, :, None], (depth, V_DIM, ATTN_TILE))
    n_pool = len(POOL_WINDOWS)
    group_eye = jnp.eye(n_pool, dtype=pool_w.dtype)
    pool_bd = jnp.einsum('lgcd,gh->lgchd', pool_w, group_eye).reshape(depth, w, w).astype(BF16)
    w_in_b = w_in[0].astype(BF16)

    for l in range(depth):
        lam_init = 0.8 - 0.6 * math.exp(-0.3 * l)
        later = [(w_out, l), (w_ff1, l), (w_ff2, l)] + ([(w_in, l + 1)] if l + 1 < depth else [])
        qt, k, vt, glu, yo, w_out_b, w_ff1_b, w_ff2_b, *w_in_next = _inproj_mixers_call(
            l, x, mods, rows(norm1_g), w_in_b, qk_gain, ones32,
            short_conv_w, pool_bd, rows(pool_scale), later)
        if w_in_next:
            w_in_b = w_in_next[0]
        ya = _attn_call(l, qt, k, vt, lam_params, sub_gain, lam_init)
        x = _outproj_mlp_call(l, x, ya, glu, yo, mods, rows(norm2_g), w_out_b, w_ff1_b, w_ff2_b,
                              conf_dw_w, rows(conf_dw_b), rows(conf_ln_g), rows(conf_ln_b))
    return x
```

```python
import functools
import math

import jax
import jax.numpy as jnp
from jax import lax
from jax.experimental import pallas as pl
from jax.experimental.pallas import tpu as pltpu

F32 = jnp.float32
BF16 = jnp.bfloat16

EPS = 1e-6
LOG2_E = math.log2(math.e)
N_MOD = 6
ATTN_HEADS = 4
QK_DIM = 32
V_DIM = 64
VT_ROWS = V_DIM + 16
N_MAPS = 2 * ATTN_HEADS
CONF_KERNEL = 31
SHORT_KERNEL = 3
POOL_WINDOWS = (2, 4, 8, 16)
GROUP_WIDTH = 256
SUBLANES = 8
LANES = 128
ANCHOR_ROWS = 16

ROW_TILE = 512
INPROJ_TILE = 1024
SUB_TILE = 256
ATTN_TILE = 256
SCORE_LEAD = 2
CONV_CHUNK = 64
HALO = 32
VMEM_LIMIT = 56 * 1024 * 1024


def _split_bf16(x):
    hi = x.astype(BF16)
    lo = (x - hi.astype(F32)).astype(BF16)
    return hi, lo


def _group_sum(x, ones_blockdiag):
    return jnp.dot(x.astype(BF16), ones_blockdiag, preferred_element_type=F32)


def _modulated_norm(x, gain, shift, scale):
    ms = jnp.mean(x * x, axis=-1, keepdims=True)
    return (x * lax.rsqrt(ms + EPS)) * (gain * (1.0 + scale)) + shift


def _ada_kernel(c_ref, w_ref, b_ref, o_ref):
    c = c_ref[...]
    c_act = c * jax.nn.sigmoid(c)
    c_hi, c_lo = _split_bf16(c_act)
    w_hi, w_lo = _split_bf16(w_ref[0])
    acc = jnp.dot(c_hi, w_hi, preferred_element_type=F32)
    acc += jnp.dot(c_lo, w_hi, preferred_element_type=F32)
    acc += jnp.dot(c_hi, w_lo, preferred_element_type=F32)
    o_ref[0] = acc + b_ref[0]


def _ada_call(c, w_ada, b_ada):
    depth, d, n = w_ada.shape
    b = c.shape[0]
    tn = n // 4
    return pl.pallas_call(
        _ada_kernel,
        grid=(depth, n // tn),
        in_specs=[
            pl.BlockSpec((b, d), lambda l, j: (0, 0)),
            pl.BlockSpec((1, d, tn), lambda l, j: (l, 0, j)),
            pl.BlockSpec((1, 1, tn), lambda l, j: (l, 0, j)),
        ],
        out_specs=pl.BlockSpec((1, b, tn), lambda l, j: (l, 0, j)),
        out_shape=jax.ShapeDtypeStruct((depth, b, n), F32),
        compiler_params=pltpu.CompilerParams(
            dimension_semantics=("arbitrary", "arbitrary"), vmem_limit_bytes=VMEM_LIMIT),
        name="ada_mod",
    )(c, w_ada, b_ada.reshape(depth, 1, n))


def _conformer_pieces(h_scr, ph_scr, cw_ref, cbias_ref, lng_ref, lnb_ref, dst_ref):
    rows, w = dst_ref.shape
    first = HALO - (CONF_KERNEL - 1)
    phases = []
    for phase in range(SUBLANES):
        taps = [k for k in range(CONF_KERNEL) if (first + k) % SUBLANES == phase]
        src = (h_scr, first + taps[0]) if phase == 0 else (ph_scr.at[phase], 0)
        phases.append((taps,) + src)

    def shifted_copies():
        for phase in range(1, SUBLANES):
            taps = phases[phase][0]
            base = first + taps[0]
            span = taps[-1] - taps[0] + rows
            ph_scr[phase, 0:span, :] = h_scr[base:base + span, :]

    def chunk(r, bias=None):
        acc = jnp.broadcast_to(cbias_ref[...] if bias is None else bias, (CONV_CHUNK, w))
        for taps, src, base in phases:
            window = src[r + base:r + base + taps[-1] - taps[0] + CONV_CHUNK, :]
            for k in taps:
                o = k - taps[0]
                acc = acc + cw_ref[k:k + 1, :] * window[o:o + CONV_CHUNK, :]
        mu = jnp.mean(acc, axis=-1, keepdims=True)
        cen = acc - mu
        var = jnp.mean(cen * cen, axis=-1, keepdims=True)
        y = cen * lax.rsqrt(var + EPS) * lng_ref[...] + lnb_ref[...]
        out = y * jax.nn.sigmoid(y)
        dst_ref[r:r + CONV_CHUNK, :] = out.astype(dst_ref.dtype)
        return out[0:ANCHOR_ROWS, 0:LANES]

    return [shifted_copies] + [functools.partial(chunk, r) for r in range(0, rows, CONV_CHUNK)]


def _inproj_mixers_kernel(*refs, n_slabs):
    (x_ref, mod_ref, g_ref, w_ref, qkg_ref, ones32_ref, sw_ref, pw_ref, ps_ref) = refs[:9]
    wide_refs = refs[9:9 + n_slabs]
    qt_ref, k_ref, vt_ref, glu_ref, yo_ref = refs[9 + n_slabs:14 + n_slabs]
    narrow_refs = refs[14 + n_slabs:14 + 2 * n_slabs]
    hb_scr, pa_scr, pb_scr, c_scr, d_scr, e_scr, f_scr = refs[14 + 2 * n_slabs:]

    for wide, narrow in zip(wide_refs, narrow_refs):
        narrow[...] = wide[0].astype(narrow.dtype)

    ts = x_ref.shape[1]
    w = GROUP_WIDTH
    n_cols = w_ref.shape[1] // w
    rows = SUB_TILE
    j = pl.program_id(1)

    @pl.when(j == 0)
    def _():
        zeros = jnp.zeros((HALO, w), F32)
        c_scr[0:HALO, :] = zeros
        d_scr[0:HALO, :] = zeros

    lane = lax.broadcasted_iota(jnp.int32, (1, w), 1)
    group = lane // (w // len(POOL_WINDOWS))
    col = lambda c: slice(c * w, (c + 1) * w)

    def matmul_pieces(r0, dst):
        def normalise():
            hb_scr[...] = _modulated_norm(x_ref[0, r0:r0 + rows, :], g_ref[0], mod_ref[0, 0, 0:1, :],
                                          mod_ref[0, 0, 1:2, :]).astype(BF16)

        def project(c):
            dst[:, col(c)] = jnp.dot(hb_scr[...], w_ref[:, col(c)], preferred_element_type=F32)

        return [normalise] + [functools.partial(project, c) for c in range(n_cols)]

    def vector_pieces(r0, src):
        tile = slice(r0, r0 + rows)
        scr = slice(HALO + r0, HALO + r0 + rows)

        def qk_norm(c):
            p = src[:, col(c)]
            ms = _group_sum(p * p, ones32_ref[...]) * (1.0 / QK_DIM)
            p = p * lax.rsqrt(ms + EPS) * qkg_ref[0, :, col(c)]
            if c == 0:
                qt_ref[0, :, tile] = p.T.astype(qt_ref.dtype)
            else:
                k_ref[0, tile, :] = p.astype(k_ref.dtype)

        def v_and_glu():
            vt_ref[0, :, tile] = src[:, col(2)].T.astype(vt_ref.dtype)
            glu_ref[0, tile, :] = src[:, col(3)] * jax.nn.sigmoid(src[:, col(4)])

        def short_conv():
            c_scr[scr, :] = src[:, col(6)] * src[:, col(7)]
            conv = None
            for k in range(SHORT_KERNEL):
                off = HALO + r0 - (SHORT_KERNEL - 1) + k
                term = sw_ref[0, k:k + 1, :] * c_scr[off:off + rows, :]
                conv = term if conv is None else conv + term
            yo_ref[0, tile, col(0)] = (src[:, col(5)] * conv).astype(yo_ref.dtype)

        def pool():
            d_scr[scr, :] = src[:, col(8)]
            end = HALO + r0 + rows
            lo = r0 + SUBLANES
            e_scr[lo:end, :] = d_scr[lo:end, :] + d_scr[lo - 1:end - 1, :]
            pooled = e_scr[scr, :]
            win = jnp.full((1, w), POOL_WINDOWS[0], jnp.int32)
            a, b = e_scr, f_scr
            for g in range(1, len(POOL_WINDOWS)):
                shift = POOL_WINDOWS[g - 1]
                lo = r0 + SUBLANES * (g + 1)
                b[lo:end, :] = a[lo:end, :] + a[lo - shift:end - shift, :]
                pooled = jnp.where(group >= g, b[scr, :], pooled)
                win = jnp.where(group >= g, POOL_WINDOWS[g], win)
                a, b = b, a
            t_pos = j * ts + r0 + lax.broadcasted_iota(jnp.int32, (rows, w), 0)
            cnt = jnp.minimum(t_pos + 1, win).astype(F32)
            y = (pooled / cnt - d_scr[scr, :]).astype(BF16)
            yd = jnp.dot(y, pw_ref[0], preferred_element_type=F32) * ps_ref[0]
            yo_ref[0, tile, col(1)] = yd.astype(yo_ref.dtype)

        return [functools.partial(qk_norm, 0), functools.partial(qk_norm, 1), v_and_glu,
                short_conv, pool]

    buffers = (pa_scr, pb_scr)
    n_sub = ts // rows
    for i in range(n_sub + 1):
        mm = matmul_pieces(i * rows, buffers[i % 2]) if i < n_sub else []
        vec = vector_pieces((i - 1) * rows, buffers[(i - 1) % 2]) if i > 0 else []
        stride = max(1, len(mm) // max(1, len(vec)))
        for k in range(max(len(mm), stride * len(vec))):
            if k < len(mm):
                mm[k]()
            if k % stride == stride - 1 and k // stride < len(vec):
                vec[k // stride]()

    for buf in (c_scr, d_scr):
        buf[0:HALO, :] = buf[ts:ts + HALO, :]


def _inproj_mixers_call(layer, x, mods, norm_g, w_in, qk_gain, ones32, short_w, pool_bd, pool_scale,
                        later_weights):
    b, s, d = x.shape
    n = w_in.shape[1]
    ts = INPROJ_TILE
    w = GROUP_WIDTH
    n_tiles = s // ts
    n_steps = b * n_tiles
    per_layer = lambda *shape: pl.BlockSpec((1,) + shape, lambda i, j: (layer,) + (0,) * len(shape))
    rows_of = lambda width: pl.BlockSpec((1, ts, width), lambda i, j: (i, j, 0))
    transposed = pl.BlockSpec((1, w, ts), lambda i, j: (i, 0, j))
    slab_in, slab_out, slab_shape = [], [], []
    for wide, wide_layer in later_weights:
        _, r, c = wide.shape
        slab_in.append(pl.BlockSpec((1, r // n_steps, c),
                                    functools.partial(lambda wl, i, j: (wl, i * n_tiles + j, 0), wide_layer)))
        slab_out.append(pl.BlockSpec((r // n_steps, c), lambda i, j: (i * n_tiles + j, 0)))
        slab_shape.append(jax.ShapeDtypeStruct((r, c), BF16))
    return pl.pallas_call(
        functools.partial(_inproj_mixers_kernel, n_slabs=len(later_weights)),
        grid=(b, s // ts),
        in_specs=[
            rows_of(d),
            pl.BlockSpec((1, 1, N_MOD, d), lambda i, j: (layer, i, 0, 0)),
            per_layer(1, d),
            pl.BlockSpec((d, n), lambda i, j: (0, 0)),
            per_layer(1, 2 * w),
            pl.BlockSpec((w, w), lambda i, j: (0, 0)),
            per_layer(SHORT_KERNEL, w), per_layer(w, w), per_layer(1, w),
        ] + slab_in,
        out_specs=[transposed, rows_of(w), transposed, rows_of(w), rows_of(2 * w)] + slab_out,
        out_shape=[jax.ShapeDtypeStruct((b, w, s), BF16),
                   jax.ShapeDtypeStruct((b, s, w), BF16),
                   jax.ShapeDtypeStruct((b, w, s), BF16),
                   jax.ShapeDtypeStruct((b, s, w), F32),
                   jax.ShapeDtypeStruct((b, s, 2 * w), BF16)] + slab_shape,
        scratch_shapes=[pltpu.VMEM((SUB_TILE, d), BF16),
                        pltpu.VMEM((SUB_TILE, n), F32), pltpu.VMEM((SUB_TILE, n), F32)]
        + [pltpu.VMEM((HALO + ts, w), F32) for _ in range(4)],
        compiler_params=pltpu.CompilerParams(
            dimension_semantics=("arbitrary", "arbitrary"), vmem_limit_bytes=VMEM_LIMIT),
        name="inproj_mixers",
    )(x, mods, norm_g, w_in, qk_gain, ones32, short_w, pool_bd, pool_scale,
      *[wide for wide, _ in later_weights])


def _attn_kernel(qa_ref, qb_ref, qn_ref, k_ref, vt_ref, lamp_ref, subg_ref, o_ref,
                 vt_scr, qma_scr, qmb_scr, qmn_scr, sa_scr, sb_scr, m_scr, acc_scr, ot_scr,
                 *, lam_init):
    t = qa_ref.shape[2]
    n_chunks = k_ref.shape[1] // t
    step = pl.program_id(1)

    @pl.when(step == 0)
    def _():
        for c in range(n_chunks):
            for head in range(ATTN_HEADS):
                r = head * VT_ROWS
                vt_scr[c, r:r + V_DIM, :] = vt_ref[0, head * V_DIM:(head + 1) * V_DIM, c * t:(c + 1) * t]
                vt_scr[c, r + V_DIM:r + VT_ROWS, :] = jnp.ones((VT_ROWS - V_DIM, t), BF16)
        for qm in (qma_scr, qmb_scr, qmn_scr):
            qm[...] = jnp.zeros(qm.shape, BF16)

    for g in range(N_MAPS):
        rows = slice(g * QK_DIM, (g + 1) * QK_DIM)
        qma_scr[g, rows, :] = qa_ref[0, rows, :]
        qmb_scr[g, rows, :] = qb_ref[0, rows, :]
        qmn_scr[g, rows, :] = qn_ref[0, rows, :]

    def fold(x, op):
        parts = [x[r:r + SUBLANES, :] for r in range(0, x.shape[0], SUBLANES)]
        while len(parts) > 1:
            parts = [op(parts[i], parts[i + 1]) for i in range(0, len(parts), 2)]
        return parts[0]

    def stage(score_jobs, soft_chunk, soft_src, diagonal=False):
        keys = [k_ref[0, pl.ds(pl.multiple_of(job[0] * t, t), t), :] for job in score_jobs]
        if diagonal:
            key = lax.broadcasted_iota(jnp.int32, (t, t), 0)
            query = lax.broadcasted_iota(jnp.int32, (t, t), 1)
            keep = key <= query

        def scores(g):
            for kb, (_, qm, dst) in zip(keys, score_jobs):
                dst[g] = jnp.dot(kb, qm[g], preferred_element_type=F32)

        for g in range(SCORE_LEAD):
            scores(g)
        for g in range(N_MAPS):
            if g + SCORE_LEAD < N_MAPS:
                scores(g + SCORE_LEAD)
            if soft_chunk is None:
                continue
            head = g // 2
            st = soft_src[g]
            if diagonal:
                st = jnp.where(keep, st, -jnp.inf)
            m_prev = m_scr[g]
            m_new = jnp.maximum(m_prev, jnp.max(fold(st, jnp.maximum), axis=0, keepdims=True))
            alpha = jnp.exp2(m_prev - m_new)
            p = jnp.exp2((st - m_new).astype(BF16))
            m_scr[g] = m_new
            vt = vt_scr[soft_chunk, head * VT_ROWS:(head + 1) * VT_ROWS, :]
            pv = jnp.dot(vt, p, preferred_element_type=F32)
            acc = acc_scr[g] * alpha + pv
            if not diagonal:
                acc_scr[g] = acc
                continue
            o = acc[0:V_DIM, :] * (1.0 / acc[V_DIM:V_DIM + 1, :])
            if g % 2 == 0:
                o_first = o
            else:
                out = o_first - lam * o
                ms = jnp.mean(out * out, axis=0, keepdims=True)
                out = out * lax.rsqrt(ms + EPS) * subg_ref[0] * (1.0 - lam_init)
                ot_scr[head * V_DIM:(head + 1) * V_DIM, :] = out

    lp = lamp_ref[0]
    lam = (jnp.exp(jnp.sum(lp[0:1] * lp[1:2], axis=-1, keepdims=True))
           - jnp.exp(jnp.sum(lp[2:3] * lp[3:4], axis=-1, keepdims=True)) + lam_init)

    def query_block(odd, qm, qm_next, out_rows):
        qi = 2 * step + odd
        m_scr[...] = jnp.full(m_scr.shape, -jnp.inf, F32)
        acc_scr[...] = jnp.zeros(acc_scr.shape, F32)

        def run(first, second):
            if not odd:
                @pl.when(step == 0)
                def _():
                    stage([(0, qm, first)], None, None)

            def body(i, carry):
                stage([(2 * i + 1, qm, second)], 2 * i, first)
                stage([(2 * i + 2, qm, first)], 2 * i + 1, second)
                return carry

            lax.fori_loop(0, step, body, 0)
            if odd:
                stage([(qi, qm, second)], qi - 1, first)
                stage([(0, qm_next, first)], qi, second, diagonal=True)
            else:
                stage([(0, qm_next, second)], qi, first, diagonal=True)

        arrives_in_b = (step + odd) % 2

        @pl.when(arrives_in_b == 0)
        def _():
            run(sa_scr, sb_scr)

        @pl.when(arrives_in_b == 1)
        def _():
            run(sb_scr, sa_scr)

        o_ref[0, out_rows, :] = ot_scr[...].T.astype(o_ref.dtype)

    query_block(0, qma_scr, qmb_scr, slice(0, t))
    query_block(1, qmb_scr, qmn_scr, slice(t, 2 * t))


def _attn_call(layer, qt, k, vt, lam_params, sub_gain, lam_init):
    b, s, _ = k.shape
    t = ATTN_TILE
    w = GROUP_WIDTH
    return pl.pallas_call(
        functools.partial(_attn_kernel, lam_init=lam_init),
        grid=(b, s // (2 * t)),
        in_specs=[
            pl.BlockSpec((1, w, t), lambda i, j: (i, 0, 2 * j)),
            pl.BlockSpec((1, w, t), lambda i, j: (i, 0, 2 * j + 1)),
            pl.BlockSpec((1, w, t), lambda i, j: (i, 0, jnp.minimum(2 * j + 2, s // t - 1))),
            pl.BlockSpec((1, s, w), lambda i, j: (i, 0, 0)),
            pl.BlockSpec((1, w, s), lambda i, j: (i, 0, 0)),
            pl.BlockSpec((1, 4, QK_DIM), lambda i, j: (layer, 0, 0)),
            pl.BlockSpec((1, V_DIM, t), lambda i, j: (layer, 0, 0)),
        ],
        out_specs=pl.BlockSpec((1, 2 * t, w), lambda i, j: (i, j, 0)),
        out_shape=jax.ShapeDtypeStruct((b, s, w), BF16),
        scratch_shapes=[
            pltpu.VMEM((s // t, ATTN_HEADS * VT_ROWS, t), BF16),
            pltpu.VMEM((N_MAPS, w, t), BF16),
            pltpu.VMEM((N_MAPS, w, t), BF16),
            pltpu.VMEM((N_MAPS, w, t), BF16),
            pltpu.VMEM((N_MAPS, t, t), F32),
            pltpu.VMEM((N_MAPS, t, t), F32),
            pltpu.VMEM((N_MAPS, 1, t), F32),
            pltpu.VMEM((N_MAPS, VT_ROWS, t), F32),
            pltpu.VMEM((w, t), F32),
        ],
        compiler_params=pltpu.CompilerParams(
            dimension_semantics=("arbitrary", "arbitrary"), vmem_limit_bytes=VMEM_LIMIT),
        name="diff_attn",
    )(qt, qt, qt, k, vt, lam_params, sub_gain)


def _outproj_mlp_kernel(x_ref, ya_ref, yo_ref, glu0_ref, glun_ref, mod_ref, g_ref,
                        wout_ref, w1_ref, w2_ref, cw_ref, cbias_ref, lng_ref, lnb_ref,
                        o_ref, x1_scr, hb_scr, a_scr, yb_scr, hw_scr, ph_scr):
    tm = x_ref.shape[1]
    w = GROUP_WIDTH
    d_ff = w1_ref.shape[1]
    j = pl.program_id(1)
    n_tiles = pl.num_programs(1)
    first_step = jnp.logical_and(pl.program_id(0) == 0, j == 0)
    mod = lambda r: mod_ref[0, 0, r:r + 1, :]

    def conformer_pieces():
        return _conformer_pieces(hw_scr, ph_scr, cw_ref.at[0], cbias_ref.at[0], lng_ref.at[0],
                                 lnb_ref.at[0], yb_scr)

    @pl.when(first_step)
    def _():
        hw_scr[0:HALO, :] = jnp.zeros((HALO, w), F32)
        hw_scr[HALO:, :] = glu0_ref[0]
        for piece in conformer_pieces():
            piece()
        hw_scr[0:HALO, :] = hw_scr[tm:, :]

    mix = jnp.dot(ya_ref[0], wout_ref[0:w, :], preferred_element_type=F32)
    mix = mix + jnp.dot(yb_scr[...], wout_ref[w:2 * w, :], preferred_element_type=F32)
    mix = mix + jnp.dot(yo_ref[0], wout_ref[2 * w:, :], preferred_element_type=F32)
    x1 = x_ref[0] + mod(2) * mix
    x1_scr[...] = x1
    hb_scr[...] = _modulated_norm(x1, g_ref[0], mod(3), mod(4)).astype(BF16)

    tail = hw_scr[0:HALO, :]
    hw_scr[0:HALO, :] = jnp.where(j == n_tiles - 1, 0.0, tail)
    hw_scr[HALO:, :] = glun_ref[0]
    pieces = conformer_pieces()
    pieces.pop(0)()

    always = j < n_tiles

    def anchor(ref, cols, slab):
        kept = ref[0:ANCHOR_ROWS, cols]
        ref[0:ANCHOR_ROWS, cols] = jnp.where(always, kept, slab.astype(kept.dtype))

    chunk = 1024
    n_chunks = d_ff // chunk
    per_chunk = len(pieces) // (2 * n_chunks)
    first_lanes = slice(0, LANES)

    def mixer_group(released_by, next_input, next_cols):
        bias = jnp.where(always, cbias_ref[0], released_by[0:1, 0:w])
        for _ in range(per_chunk):
            anchor(next_input, next_cols, pieces.pop(0)(bias))

    released_by = x1
    for ci in range(n_chunks):
        c = ci * chunk
        a = jnp.dot(hb_scr[...], w1_ref[:, c:c + chunk], preferred_element_type=F32)
        a_scr[:, c:c + chunk] = jnp.square(jnp.maximum(a, 0.0)).astype(BF16)
        if ci + 1 < n_chunks:
            mixer_group(released_by, hb_scr, first_lanes)
        else:
            mixer_group(released_by, a_scr, first_lanes)
        released_by = a
    y = None
    for ci in range(n_chunks):
        c = ci * chunk
        part = jnp.dot(a_scr[:, c:c + chunk], w2_ref[c:c + chunk, :], preferred_element_type=F32)
        y = part if y is None else y + part
        if ci + 1 < n_chunks:
            mixer_group(released_by, a_scr, slice(c + chunk, c + chunk + LANES))
        else:
            mixer_group(released_by, x1_scr, first_lanes)
        released_by = part
    for piece in pieces:
        piece()
    hw_scr[0:HALO, :] = hw_scr[tm:, :]
    o_ref[0] = x1_scr[...] + mod(5) * y


def _outproj_mlp_call(layer, x, ya, glu, yo, mods, norm_g, w_out, w1, w2, conf_w, conf_b, ln_g, ln_b):
    b, s, d = x.shape
    d_ff = w1.shape[1]
    tm = ROW_TILE
    w = GROUP_WIDTH
    n_tiles = s // tm
    resident = lambda arr: pl.BlockSpec(arr.shape, lambda i, j: (0, 0), pipeline_mode=pl.Buffered(1))
    per_layer = lambda *shape: pl.BlockSpec((1,) + shape, lambda i, j: (layer,) + (0,) * len(shape))
    rows_of = lambda width: pl.BlockSpec((1, tm, width), lambda i, j: (i, j, 0))

    def next_tile(i, j):
        t = jnp.minimum(i * n_tiles + j + 1, b * n_tiles - 1)
        return (t // n_tiles, t % n_tiles, 0)

    return pl.pallas_call(
        _outproj_mlp_kernel,
        grid=(b, n_tiles),
        in_specs=[
            rows_of(d), rows_of(ya.shape[2]), rows_of(yo.shape[2]),
            pl.BlockSpec((1, tm, w), lambda i, j: (0, 0, 0)),
            pl.BlockSpec((1, tm, w), next_tile),
            pl.BlockSpec((1, 1, N_MOD, d), lambda i, j: (layer, i, 0, 0)),
            per_layer(1, d),
            resident(w_out), resident(w1), resident(w2),
            per_layer(CONF_KERNEL, w), per_layer(1, w), per_layer(1, w), per_layer(1, w),
        ],
        out_specs=rows_of(d),
        out_shape=jax.ShapeDtypeStruct((b, s, d), F32),
        scratch_shapes=[
            pltpu.VMEM((tm, d), F32),
            pltpu.VMEM((tm, d), BF16),
            pltpu.VMEM((tm, d_ff), BF16),
            pltpu.VMEM((tm, w), BF16),
            pltpu.VMEM((HALO + tm, w), F32),
            pltpu.VMEM((SUBLANES, HALO + tm, w), F32),
        ],
        compiler_params=pltpu.CompilerParams(
            dimension_semantics=("arbitrary", "arbitrary"), vmem_limit_bytes=VMEM_LIMIT),
        name="outproj_mlp",
    )(x, ya, yo, glu, glu, mods, norm_g, w_out, w1, w2, conf_w, conf_b, ln_g, ln_b)


def _block_diag_ones(width, group):
    idx = jnp.arange(width) // group
    return (idx[:, None] == idx[None, :]).astype(BF16)


def kernel(x, c, w_ada, b_ada, norm1_g, norm2_g, w_in, w_out, q_norm_g, k_norm_g, lam_params,
           attn_sub_g, conf_dw_w, conf_dw_b, conf_ln_g, conf_ln_b, short_conv_w, pool_w, pool_scale,
           w_ff1, w_ff2):
    depth, d, _ = w_in.shape
    b = x.shape[0]
    w = GROUP_WIDTH
    mods = _ada_call(c, w_ada, b_ada).reshape(depth, b, N_MOD, d)
    ones32 = _block_diag_ones(w, QK_DIM)
    rows = lambda v: v.reshape(depth, 1, -1)

    qk_gain = jnp.concatenate([jnp.tile(q_norm_g, (1, N_MAPS)) * (QK_DIM ** -0.5 * LOG2_E),
                               jnp.tile(k_norm_g, (1, N_MAPS))], axis=1).reshape(depth, 1, 2 * w)
    sub_gain = jnp.broadcast_to(attn_sub_g[:, :, None], (depth, V_DIM, ATTN_TILE))
    n_pool = len(POOL_WINDOWS)
    group_eye = jnp.eye(n_pool, dtype=pool_w.dtype)
    pool_bd = jnp.einsum('lgcd,gh->lgchd', pool_w, group_eye).reshape(depth, w, w).astype(BF16)
    w_in_b = w_in[0].astype(BF16)

    for l in range(depth):
        lam_init = 0.8 - 0.6 * math.exp(-0.3 * l)
        later = [(w_out, l), (w_ff1, l), (w_ff2, l)] + ([(w_in, l + 1)] if l + 1 < depth else [])
        qt, k, vt, glu, yo, w_out_b, w_ff1_b, w_ff2_b, *w_in_next = _inproj_mixers_call(
            l, x, mods, rows(norm1_g), w_in_b, qk_gain, ones32,
            short_conv_w, pool_bd, rows(pool_scale), later)
        if w_in_next:
            w_in_b = w_in_next[0]
        ya = _attn_call(l, qt, k, vt, lam_params, sub_gain, lam_init)
        x = _outproj_mlp_call(l, x, ya, glu, yo, mods, rows(norm2_g), w_out_b, w_ff1_b, w_ff2_b,
                              conf_dw_w, rows(conf_dw_b), rows(conf_ln_g), rows(conf_ln_b))
    return x
```

```python
import functools
import math

import jax
import jax.numpy as jnp
from jax import lax
from jax.experimental import pallas as pl
from jax.experimental.pallas import tpu as pltpu

F32 = jnp.float32
BF16 = jnp.bfloat16

EPS = 1e-6
LOG2_E = math.log2(math.e)
N_MOD = 6
ATTN_HEADS = 4
QK_DIM = 32
V_DIM = 64
VT_ROWS = V_DIM + 16
N_MAPS = 2 * ATTN_HEADS
CONF_KERNEL = 31
SHORT_KERNEL = 3
POOL_WINDOWS = (2, 4, 8, 16)
GROUP_WIDTH = 256
SUBLANES = 8
LANES = 128
ANCHOR_ROWS = 16

ROW_TILE = 512
INPROJ_TILE = 1024
SUB_TILE = 256
ATTN_TILE = 256
SCORE_LEAD = 2
MAX_SAFE_SCORE_BOUND = 60.0
CONV_CHUNK = 64
HALO = 32
VMEM_LIMIT = 56 * 1024 * 1024


def _split_bf16(x):
    hi = x.astype(BF16)
    lo = (x - hi.astype(F32)).astype(BF16)
    return hi, lo


def _group_sum(x, ones_blockdiag):
    return jnp.dot(x.astype(BF16), ones_blockdiag, preferred_element_type=F32)


def _modulated_norm(x, gain, shift, scale):
    ms = jnp.mean(x * x, axis=-1, keepdims=True)
    return (x * lax.rsqrt(ms + EPS)) * (gain * (1.0 + scale)) + shift


def _ada_kernel(c_ref, w_ref, b_ref, o_ref):
    c = c_ref[...]
    c_act = c * jax.nn.sigmoid(c)
    c_hi, c_lo = _split_bf16(c_act)
    w_hi, w_lo = _split_bf16(w_ref[0])
    acc = jnp.dot(c_hi, w_hi, preferred_element_type=F32)
    acc += jnp.dot(c_lo, w_hi, preferred_element_type=F32)
    acc += jnp.dot(c_hi, w_lo, preferred_element_type=F32)
    o_ref[0] = acc + b_ref[0]


def _ada_call(c, w_ada, b_ada):
    depth, d, n = w_ada.shape
    b = c.shape[0]
    tn = n // 4
    return pl.pallas_call(
        _ada_kernel,
        grid=(depth, n // tn),
        in_specs=[
            pl.BlockSpec((b, d), lambda l, j: (0, 0)),
            pl.BlockSpec((1, d, tn), lambda l, j: (l, 0, j)),
            pl.BlockSpec((1, 1, tn), lambda l, j: (l, 0, j)),
        ],
        out_specs=pl.BlockSpec((1, b, tn), lambda l, j: (l, 0, j)),
        out_shape=jax.ShapeDtypeStruct((depth, b, n), F32),
        compiler_params=pltpu.CompilerParams(
            dimension_semantics=("arbitrary", "arbitrary"), vmem_limit_bytes=VMEM_LIMIT),
        name="ada_mod",
    )(c, w_ada, b_ada.reshape(depth, 1, n))


def _conformer_pieces(h_scr, ph_scr, cw_ref, cbias_ref, lng_ref, lnb_ref, dst_ref):
    rows, w = dst_ref.shape
    first = HALO - (CONF_KERNEL - 1)
    phases = []
    for phase in range(SUBLANES):
        taps = [k for k in range(CONF_KERNEL) if (first + k) % SUBLANES == phase]
        src = (h_scr, first + taps[0]) if phase == 0 else (ph_scr.at[phase], 0)
        phases.append((taps,) + src)

    def shifted_copies():
        for phase in range(1, SUBLANES):
            taps = phases[phase][0]
            base = first + taps[0]
            span = taps[-1] - taps[0] + rows
            ph_scr[phase, 0:span, :] = h_scr[base:base + span, :]

    def chunk(r, bias=None):
        acc = jnp.broadcast_to(cbias_ref[...] if bias is None else bias, (CONV_CHUNK, w))
        for taps, src, base in phases:
            window = src[r + base:r + base + taps[-1] - taps[0] + CONV_CHUNK, :]
            for k in taps:
                o = k - taps[0]
                acc = acc + cw_ref[k:k + 1, :] * window[o:o + CONV_CHUNK, :]
        mu = jnp.mean(acc, axis=-1, keepdims=True)
        cen = acc - mu
        var = jnp.mean(cen * cen, axis=-1, keepdims=True)
        y = cen * lax.rsqrt(var + EPS) * lng_ref[...] + lnb_ref[...]
        out = y * jax.nn.sigmoid(y)
        dst_ref[r:r + CONV_CHUNK, :] = out.astype(dst_ref.dtype)
        return out[0:ANCHOR_ROWS, 0:LANES]

    return [shifted_copies] + [functools.partial(chunk, r) for r in range(0, rows, CONV_CHUNK)]


def _inproj_mixers_kernel(*refs, n_slabs):
    (x_ref, mod_ref, g_ref, w_ref, qkg_ref, ones32_ref, sw_ref, pw_ref, ps_ref) = refs[:9]
    wide_refs = refs[9:9 + n_slabs]
    qt_ref, k_ref, vt_ref, glu_ref, yo_ref = refs[9 + n_slabs:14 + n_slabs]
    narrow_refs = refs[14 + n_slabs:14 + 2 * n_slabs]
    hb_scr, pa_scr, pb_scr, c_scr, d_scr, e_scr, f_scr = refs[14 + 2 * n_slabs:]

    for wide, narrow in zip(wide_refs, narrow_refs):
        narrow[...] = wide[0].astype(narrow.dtype)

    ts = x_ref.shape[1]
    w = GROUP_WIDTH
    n_cols = w_ref.shape[1] // w
    rows = SUB_TILE
    j = pl.program_id(1)

    @pl.when(j == 0)
    def _():
        zeros = jnp.zeros((HALO, w), F32)
        c_scr[0:HALO, :] = zeros
        d_scr[0:HALO, :] = zeros

    lane = lax.broadcasted_iota(jnp.int32, (1, w), 1)
    group = lane // (w // len(POOL_WINDOWS))
    col = lambda c: slice(c * w, (c + 1) * w)

    def matmul_pieces(r0, dst):
        def normalise():
            hb_scr[...] = _modulated_norm(x_ref[0, r0:r0 + rows, :], g_ref[0], mod_ref[0, 0, 0:1, :],
                                          mod_ref[0, 0, 1:2, :]).astype(BF16)

        def project(c):
            dst[:, col(c)] = jnp.dot(hb_scr[...], w_ref[:, col(c)], preferred_element_type=F32)

        return [normalise] + [functools.partial(project, c) for c in range(n_cols)]

    def vector_pieces(r0, src):
        tile = slice(r0, r0 + rows)
        scr = slice(HALO + r0, HALO + r0 + rows)

        def qk_norm(c):
            p = src[:, col(c)]
            ms = _group_sum(p * p, ones32_ref[...]) * (1.0 / QK_DIM)
            p = p * lax.rsqrt(ms + EPS) * qkg_ref[0, :, col(c)]
            if c == 0:
                qt_ref[0, :, tile] = p.T.astype(qt_ref.dtype)
            else:
                k_ref[0, tile, :] = p.astype(k_ref.dtype)

        def v_and_glu():
            vt_ref[0, :, tile] = src[:, col(2)].T.astype(vt_ref.dtype)
            glu_ref[0, tile, :] = src[:, col(3)] * jax.nn.sigmoid(src[:, col(4)])

        def short_conv():
            c_scr[scr, :] = src[:, col(6)] * src[:, col(7)]
            conv = None
            for k in range(SHORT_KERNEL):
                off = HALO + r0 - (SHORT_KERNEL - 1) + k
                term = sw_ref[0, k:k + 1, :] * c_scr[off:off + rows, :]
                conv = term if conv is None else conv + term
            yo_ref[0, tile, col(0)] = (src[:, col(5)] * conv).astype(yo_ref.dtype)

        def pool():
            d_scr[scr, :] = src[:, col(8)]
            end = HALO + r0 + rows
            lo = r0 + SUBLANES
            e_scr[lo:end, :] = d_scr[lo:end, :] + d_scr[lo - 1:end - 1, :]
            pooled = e_scr[scr, :]
            win = jnp.full((1, w), POOL_WINDOWS[0], jnp.int32)
            a, b = e_scr, f_scr
            for g in range(1, len(POOL_WINDOWS)):
                shift = POOL_WINDOWS[g - 1]
                lo = r0 + SUBLANES * (g + 1)
                b[lo:end, :] = a[lo:end, :] + a[lo - shift:end - shift, :]
                pooled = jnp.where(group >= g, b[scr, :], pooled)
                win = jnp.where(group >= g, POOL_WINDOWS[g], win)
                a, b = b, a
            t_pos = j * ts + r0 + lax.broadcasted_iota(jnp.int32, (rows, w), 0)
            cnt = jnp.minimum(t_pos + 1, win).astype(F32)
            y = (pooled / cnt - d_scr[scr, :]).astype(BF16)
            yd = jnp.dot(y, pw_ref[0], preferred_element_type=F32) * ps_ref[0]
            yo_ref[0, tile, col(1)] = yd.astype(yo_ref.dtype)

        return [functools.partial(qk_norm, 0), functools.partial(qk_norm, 1), v_and_glu,
                short_conv, pool]

    buffers = (pa_scr, pb_scr)
    n_sub = ts // rows
    for i in range(n_sub + 1):
        mm = matmul_pieces(i * rows, buffers[i % 2]) if i < n_sub else []
        vec = vector_pieces((i - 1) * rows, buffers[(i - 1) % 2]) if i > 0 else []
        stride = max(1, len(mm) // max(1, len(vec)))
        for k in range(max(len(mm), stride * len(vec))):
            if k < len(mm):
                mm[k]()
            if k % stride == stride - 1 and k // stride < len(vec):
                vec[k // stride]()

    for buf in (c_scr, d_scr):
        buf[0:HALO, :] = buf[ts:ts + HALO, :]


def _inproj_mixers_call(layer, x, mods, norm_g, w_in, qk_gain, ones32, short_w, pool_bd, pool_scale,
                        later_weights):
    b, s, d = x.shape
    n = w_in.shape[1]
    ts = INPROJ_TILE
    w = GROUP_WIDTH
    n_tiles = s // ts
    n_steps = b * n_tiles
    per_layer = lambda *shape: pl.BlockSpec((1,) + shape, lambda i, j: (layer,) + (0,) * len(shape))
    rows_of = lambda width: pl.BlockSpec((1, ts, width), lambda i, j: (i, j, 0))
    transposed = pl.BlockSpec((1, w, ts), lambda i, j: (i, 0, j))
    slab_in, slab_out, slab_shape = [], [], []
    for wide, wide_layer in later_weights:
        _, r, c = wide.shape
        slab_in.append(pl.BlockSpec((1, r // n_steps, c),
                                    functools.partial(lambda wl, i, j: (wl, i * n_tiles + j, 0), wide_layer)))
        slab_out.append(pl.BlockSpec((r // n_steps, c), lambda i, j: (i * n_tiles + j, 0)))
        slab_shape.append(jax.ShapeDtypeStruct((r, c), BF16))
    return pl.pallas_call(
        functools.partial(_inproj_mixers_kernel, n_slabs=len(later_weights)),
        grid=(b, s // ts),
        in_specs=[
            rows_of(d),
            pl.BlockSpec((1, 1, N_MOD, d), lambda i, j: (layer, i, 0, 0)),
            per_layer(1, d),
            pl.BlockSpec((d, n), lambda i, j: (0, 0)),
            per_layer(1, 2 * w),
            pl.BlockSpec((w, w), lambda i, j: (0, 0)),
            per_layer(SHORT_KERNEL, w), per_layer(w, w), per_layer(1, w),
        ] + slab_in,
        out_specs=[transposed, rows_of(w), transposed, rows_of(w), rows_of(2 * w)] + slab_out,
        out_shape=[jax.ShapeDtypeStruct((b, w, s), BF16),
                   jax.ShapeDtypeStruct((b, s, w), BF16),
                   jax.ShapeDtypeStruct((b, w, s), BF16),
                   jax.ShapeDtypeStruct((b, s, w), F32),
                   jax.ShapeDtypeStruct((b, s, 2 * w), BF16)] + slab_shape,
        scratch_shapes=[pltpu.VMEM((SUB_TILE, d), BF16),
                        pltpu.VMEM((SUB_TILE, n), F32), pltpu.VMEM((SUB_TILE, n), F32)]
        + [pltpu.VMEM((HALO + ts, w), F32) for _ in range(4)],
        compiler_params=pltpu.CompilerParams(
            dimension_semantics=("arbitrary", "arbitrary"), vmem_limit_bytes=VMEM_LIMIT),
        name="inproj_mixers",
    )(x, mods, norm_g, w_in, qk_gain, ones32, short_w, pool_bd, pool_scale,
      *[wide for wide, _ in later_weights])


def _attn_kernel(bound_ref, qa_ref, qb_ref, qn_ref, k_ref, vt_ref, lamp_ref, subg_ref, o_ref,
                 vt_scr, qma_scr, qmb_scr, qmn_scr, sa_scr, sb_scr, m_scr, acc_scr, ot_scr,
                 *, layer, lam_init):
    t = qa_ref.shape[2]
    n_chunks = k_ref.shape[1] // t
    step = pl.program_id(1)
    score_bound = bound_ref[layer]
    bound_is_safe = score_bound <= MAX_SAFE_SCORE_BOUND

    @pl.when(step == 0)
    def _():
        for c in range(n_chunks):
            for head in range(ATTN_HEADS):
                r = head * VT_ROWS
                vt_scr[c, r:r + V_DIM, :] = vt_ref[0, head * V_DIM:(head + 1) * V_DIM, c * t:(c + 1) * t]
                vt_scr[c, r + V_DIM:r + VT_ROWS, :] = jnp.ones((VT_ROWS - V_DIM, t), BF16)
        for qm in (qma_scr, qmb_scr, qmn_scr):
            qm[...] = jnp.zeros(qm.shape, BF16)

    for g in range(N_MAPS):
        rows = slice(g * QK_DIM, (g + 1) * QK_DIM)
        qma_scr[g, rows, :] = qa_ref[0, rows, :]
        qmb_scr[g, rows, :] = qb_ref[0, rows, :]
        qmn_scr[g, rows, :] = qn_ref[0, rows, :]

    def fold(x, op):
        parts = [x[r:r + SUBLANES, :] for r in range(0, x.shape[0], SUBLANES)]
        while len(parts) > 1:
            parts = [op(parts[i], parts[i + 1]) for i in range(0, len(parts), 2)]
        return parts[0]

    def stage(bounded, score_jobs, soft_chunk, soft_src, diagonal=False):
        keys = [k_ref[0, pl.ds(pl.multiple_of(job[0] * t, t), t), :] for job in score_jobs]
        if diagonal:
            key = lax.broadcasted_iota(jnp.int32, (t, t), 0)
            query = lax.broadcasted_iota(jnp.int32, (t, t), 1)
            keep = key <= query

        def scores(g):
            for kb, (_, qm, dst) in zip(keys, score_jobs):
                dst[g] = jnp.dot(kb, qm[g], preferred_element_type=F32)

        for g in range(SCORE_LEAD):
            scores(g)
        for g in range(N_MAPS):
            if g + SCORE_LEAD < N_MAPS:
                scores(g + SCORE_LEAD)
            if soft_chunk is None:
                continue
            head = g // 2
            st = soft_src[g]
            if diagonal:
                st = jnp.where(keep, st, -jnp.inf)
            vt = vt_scr[soft_chunk, head * VT_ROWS:(head + 1) * VT_ROWS, :]
            if bounded:
                p = jnp.exp2(st - score_bound).astype(BF16)
                acc = acc_scr[g] + jnp.dot(vt, p, preferred_element_type=F32)
            else:
                m_prev = m_scr[g]
                m_new = jnp.maximum(m_prev, jnp.max(fold(st, jnp.maximum), axis=0, keepdims=True))
                alpha = jnp.exp2(m_prev - m_new)
                p = jnp.exp2((st - m_new).astype(BF16))
                m_scr[g] = m_new
                pv = jnp.dot(vt, p, preferred_element_type=F32)
                acc = acc_scr[g] * alpha + pv
            if not diagonal:
                acc_scr[g] = acc
                continue
            o = acc[0:V_DIM, :] * (1.0 / acc[V_DIM:V_DIM + 1, :])
            if g % 2 == 0:
                o_first = o
            else:
                out = o_first - lam * o
                ms = jnp.mean(out * out, axis=0, keepdims=True)
                out = out * lax.rsqrt(ms + EPS) * subg_ref[0] * (1.0 - lam_init)
                ot_scr[head * V_DIM:(head + 1) * V_DIM, :] = out

    lp = lamp_ref[0]
    lam = (jnp.exp(jnp.sum(lp[0:1] * lp[1:2], axis=-1, keepdims=True))
           - jnp.exp(jnp.sum(lp[2:3] * lp[3:4], axis=-1, keepdims=True)) + lam_init)

    def query_block(odd, qm, qm_next, out_rows):
        qi = 2 * step + odd
        m_scr[...] = jnp.full(m_scr.shape, -jnp.inf, F32)
        acc_scr[...] = jnp.zeros(acc_scr.shape, F32)

        def run(bounded, first, second):
            if not odd:
                @pl.when(step == 0)
                def _():
                    stage(bounded, [(0, qm, first)], None, None)

            def body(i, carry):
                stage(bounded, [(2 * i + 1, qm, second)], 2 * i, first)
                stage(bounded, [(2 * i + 2, qm, first)], 2 * i + 1, second)
                return carry

            lax.fori_loop(0, step, body, 0)
            if odd:
                stage(bounded, [(qi, qm, second)], qi - 1, first)
                stage(bounded, [(0, qm_next, first)], qi, second, diagonal=True)
            else:
                stage(bounded, [(0, qm_next, second)], qi, first, diagonal=True)

        arrives_in_b = (step + odd) % 2
        for bounded in (True, False):
            chosen = bound_is_safe if bounded else jnp.logical_not(bound_is_safe)

            @pl.when(jnp.logical_and(chosen, arrives_in_b == 0))
            def _():
                run(bounded, sa_scr, sb_scr)

            @pl.when(jnp.logical_and(chosen, arrives_in_b == 1))
            def _():
                run(bounded, sb_scr, sa_scr)

        o_ref[0, out_rows, :] = ot_scr[...].T.astype(o_ref.dtype)

    query_block(0, qma_scr, qmb_scr, slice(0, t))
    query_block(1, qmb_scr, qmn_scr, slice(t, 2 * t))


def _attn_call(layer, score_bounds, qt, k, vt, lam_params, sub_gain, lam_init):
    b, s, _ = k.shape
    t = ATTN_TILE
    w = GROUP_WIDTH
    return pl.pallas_call(
        functools.partial(_attn_kernel, layer=layer, lam_init=lam_init),
        grid=(b, s // (2 * t)),
        in_specs=[
            pl.BlockSpec(memory_space=pltpu.SMEM),
            pl.BlockSpec((1, w, t), lambda i, j: (i, 0, 2 * j)),
            pl.BlockSpec((1, w, t), lambda i, j: (i, 0, 2 * j + 1)),
            pl.BlockSpec((1, w, t), lambda i, j: (i, 0, jnp.minimum(2 * j + 2, s // t - 1))),
            pl.BlockSpec((1, s, w), lambda i, j: (i, 0, 0)),
            pl.BlockSpec((1, w, s), lambda i, j: (i, 0, 0)),
            pl.BlockSpec((1, 4, QK_DIM), lambda i, j: (layer, 0, 0)),
            pl.BlockSpec((1, V_DIM, t), lambda i, j: (layer, 0, 0)),
        ],
        out_specs=pl.BlockSpec((1, 2 * t, w), lambda i, j: (i, j, 0)),
        out_shape=jax.ShapeDtypeStruct((b, s, w), BF16),
        scratch_shapes=[
            pltpu.VMEM((s // t, ATTN_HEADS * VT_ROWS, t), BF16),
            pltpu.VMEM((N_MAPS, w, t), BF16),
            pltpu.VMEM((N_MAPS, w, t), BF16),
            pltpu.VMEM((N_MAPS, w, t), BF16),
            pltpu.VMEM((N_MAPS, t, t), F32),
            pltpu.VMEM((N_MAPS, t, t), F32),
            pltpu.VMEM((N_MAPS, 1, t), F32),
            pltpu.VMEM((N_MAPS, VT_ROWS, t), F32),
            pltpu.VMEM((w, t), F32),
        ],
        compiler_params=pltpu.CompilerParams(
            dimension_semantics=("arbitrary", "arbitrary"), vmem_limit_bytes=VMEM_LIMIT),
        name="diff_attn",
    )(score_bounds, qt, qt, qt, k, vt, lam_params, sub_gain)


def _outproj_mlp_kernel(x_ref, ya_ref, yo_ref, glu0_ref, glun_ref, mod_ref, g_ref,
                        wout_ref, w1_ref, w2_ref, cw_ref, cbias_ref, lng_ref, lnb_ref,
                        o_ref, x1_scr, hb_scr, a_scr, yb_scr, hw_scr, ph_scr):
    tm = x_ref.shape[1]
    w = GROUP_WIDTH
    d_ff = w1_ref.shape[1]
    j = pl.program_id(1)
    n_tiles = pl.num_programs(1)
    first_step = jnp.logical_and(pl.program_id(0) == 0, j == 0)
    mod = lambda r: mod_ref[0, 0, r:r + 1, :]

    def conformer_pieces():
        return _conformer_pieces(hw_scr, ph_scr, cw_ref.at[0], cbias_ref.at[0], lng_ref.at[0],
                                 lnb_ref.at[0], yb_scr)

    @pl.when(first_step)
    def _():
        hw_scr[0:HALO, :] = jnp.zeros((HALO, w), F32)
        hw_scr[HALO:, :] = glu0_ref[0]
        for piece in conformer_pieces():
            piece()
        hw_scr[0:HALO, :] = hw_scr[tm:, :]

    mix = jnp.dot(ya_ref[0], wout_ref[0:w, :], preferred_element_type=F32)
    mix = mix + jnp.dot(yb_scr[...], wout_ref[w:2 * w, :], preferred_element_type=F32)
    mix = mix + jnp.dot(yo_ref[0], wout_ref[2 * w:, :], preferred_element_type=F32)
    x1 = x_ref[0] + mod(2) * mix
    x1_scr[...] = x1
    hb_scr[...] = _modulated_norm(x1, g_ref[0], mod(3), mod(4)).astype(BF16)

    tail = hw_scr[0:HALO, :]
    hw_scr[0:HALO, :] = jnp.where(j == n_tiles - 1, 0.0, tail)
    hw_scr[HALO:, :] = glun_ref[0]
    pieces = conformer_pieces()
    pieces.pop(0)()

    always = j < n_tiles

    def anchor(ref, cols, slab):
        kept = ref[0:ANCHOR_ROWS, cols]
        ref[0:ANCHOR_ROWS, cols] = jnp.where(always, kept, slab.astype(kept.dtype))

    chunk = 1024
    n_chunks = d_ff // chunk
    per_chunk = len(pieces) // (2 * n_chunks)
    first_lanes = slice(0, LANES)

    def mixer_group(released_by, next_input, next_cols):
        bias = jnp.where(always, cbias_ref[0], released_by[0:1, 0:w])
        for _ in range(per_chunk):
            anchor(next_input, next_cols, pieces.pop(0)(bias))

    released_by = x1
    for ci in range(n_chunks):
        c = ci * chunk
        a = jnp.dot(hb_scr[...], w1_ref[:, c:c + chunk], preferred_element_type=F32)
        a_scr[:, c:c + chunk] = jnp.square(jnp.maximum(a, 0.0)).astype(BF16)
        if ci + 1 < n_chunks:
            mixer_group(released_by, hb_scr, first_lanes)
        else:
            mixer_group(released_by, a_scr, first_lanes)
        released_by = a
    y = None
    for ci in range(n_chunks):
        c = ci * chunk
        part = jnp.dot(a_scr[:, c:c + chunk], w2_ref[c:c + chunk, :], preferred_element_type=F32)
        y = part if y is None else y + part
        if ci + 1 < n_chunks:
            mixer_group(released_by, a_scr, slice(c + chunk, c + chunk + LANES))
        else:
            mixer_group(released_by, x1_scr, first_lanes)
        released_by = part
    for piece in pieces:
        piece()
    hw_scr[0:HALO, :] = hw_scr[tm:, :]
    o_ref[0] = x1_scr[...] + mod(5) * y


def _outproj_mlp_call(layer, x, ya, glu, yo, mods, norm_g, w_out, w1, w2, conf_w, conf_b, ln_g, ln_b):
    b, s, d = x.shape
    d_ff = w1.shape[1]
    tm = ROW_TILE
    w = GROUP_WIDTH
    n_tiles = s // tm
    resident = lambda arr: pl.BlockSpec(arr.shape, lambda i, j: (0, 0), pipeline_mode=pl.Buffered(1))
    per_layer = lambda *shape: pl.BlockSpec((1,) + shape, lambda i, j: (layer,) + (0,) * len(shape))
    rows_of = lambda width: pl.BlockSpec((1, tm, width), lambda i, j: (i, j, 0))

    def next_tile(i, j):
        t = jnp.minimum(i * n_tiles + j + 1, b * n_tiles - 1)
        return (t // n_tiles, t % n_tiles, 0)

    return pl.pallas_call(
        _outproj_mlp_kernel,
        grid=(b, n_tiles),
        in_specs=[
            rows_of(d), rows_of(ya.shape[2]), rows_of(yo.shape[2]),
            pl.BlockSpec((1, tm, w), lambda i, j: (0, 0, 0)),
            pl.BlockSpec((1, tm, w), next_tile),
            pl.BlockSpec((1, 1, N_MOD, d), lambda i, j: (layer, i, 0, 0)),
            per_layer(1, d),
            resident(w_out), resident(w1), resident(w2),
            per_layer(CONF_KERNEL, w), per_layer(1, w), per_layer(1, w), per_layer(1, w),
        ],
        out_specs=rows_of(d),
        out_shape=jax.ShapeDtypeStruct((b, s, d), F32),
        scratch_shapes=[
            pltpu.VMEM((tm, d), F32),
            pltpu.VMEM((tm, d), BF16),
            pltpu.VMEM((tm, d_ff), BF16),
            pltpu.VMEM((tm, w), BF16),
            pltpu.VMEM((HALO + tm, w), F32),
            pltpu.VMEM((SUBLANES, HALO + tm, w), F32),
        ],
        compiler_params=pltpu.CompilerParams(
            dimension_semantics=("arbitrary", "arbitrary"), vmem_limit_bytes=VMEM_LIMIT),
        name="outproj_mlp",
    )(x, ya, yo, glu, glu, mods, norm_g, w_out, w1, w2, conf_w, conf_b, ln_g, ln_b)


def _block_diag_ones(width, group):
    idx = jnp.arange(width) // group
    return (idx[:, None] == idx[None, :]).astype(BF16)


def kernel(x, c, w_ada, b_ada, norm1_g, norm2_g, w_in, w_out, q_norm_g, k_norm_g, lam_params,
           attn_sub_g, conf_dw_w, conf_dw_b, conf_ln_g, conf_ln_b, short_conv_w, pool_w, pool_scale,
           w_ff1, w_ff2):
    depth, d, _ = w_in.shape
    b = x.shape[0]
    w = GROUP_WIDTH
    mods = _ada_call(c, w_ada, b_ada).reshape(depth, b, N_MOD, d)
    ones32 = _block_diag_ones(w, QK_DIM)
    rows = lambda v: v.reshape(depth, 1, -1)

    qk_gain = jnp.concatenate([jnp.tile(q_norm_g, (1, N_MAPS)) * (QK_DIM ** -0.5 * LOG2_E),
                               jnp.tile(k_norm_g, (1, N_MAPS))], axis=1).reshape(depth, 1, 2 * w)
    sub_gain = jnp.broadcast_to(attn_sub_g[:, :, None], (depth, V_DIM, ATTN_TILE))
    score_bounds = (1.02 * QK_DIM * (QK_DIM ** -0.5 * LOG2_E)
                    * jnp.max(jnp.abs(q_norm_g), axis=1) * jnp.max(jnp.abs(k_norm_g), axis=1))
    n_pool = len(POOL_WINDOWS)
    group_eye = jnp.eye(n_pool, dtype=pool_w.dtype)
    pool_bd = jnp.einsum('lgcd,gh->lgchd', pool_w, group_eye).reshape(depth, w, w).astype(BF16)
    w_in_b = w_in[0].astype(BF16)

    for l in range(depth):
        lam_init = 0.8 - 0.6 * math.exp(-0.3 * l)
        later = [(w_out, l), (w_ff1, l), (w_ff2, l)] + ([(w_in, l + 1)] if l + 1 < depth else [])
        qt, k, vt, glu, yo, w_out_b, w_ff1_b, w_ff2_b, *w_in_next = _inproj_mixers_call(
            l, x, mods, rows(norm1_g), w_in_b, qk_gain, ones32,
            short_conv_w, pool_bd, rows(pool_scale), later)
        if w_in_next:
            w_in_b = w_in_next[0]
        ya = _attn_call(l, score_bounds, qt, k, vt, lam_params, sub_gain, lam_init)
        x = _outproj_mlp_call(l, x, ya, glu, yo, mods, rows(norm2_g), w_out_b, w_ff1_b, w_ff2_b,
                              conf_dw_w, rows(conf_dw_b), rows(conf_ln_g), rows(conf_ln_b))
    return x
```

```python
import functools
import math

import jax
import jax.numpy as jnp
from jax import lax
from jax.experimental import pallas as pl
from jax.experimental.pallas import tpu as pltpu

F32 = jnp.float32
BF16 = jnp.bfloat16

EPS = 1e-6
LOG2_E = math.log2(math.e)
N_MOD = 6
ATTN_HEADS = 4
QK_DIM = 32
V_DIM = 64
VT_ROWS = V_DIM + 16
N_MAPS = 2 * ATTN_HEADS
CONF_KERNEL = 31
SHORT_KERNEL = 3
POOL_WINDOWS = (2, 4, 8, 16)
GROUP_WIDTH = 256
SUBLANES = 8
LANES = 128
ANCHOR_ROWS = 16

ROW_TILE = 512
INPROJ_TILE = 1024
SUB_TILE = 256
ATTN_TILE = 256
SCORE_LEAD = 2
MAX_SAFE_SCORE_BOUND = 60.0
CONV_CHUNK = 64
HALO = 32
VMEM_LIMIT = 56 * 1024 * 1024


def _split_bf16(x):
    hi = x.astype(BF16)
    lo = (x - hi.astype(F32)).astype(BF16)
    return hi, lo


def _group_sum(x, ones_blockdiag):
    return jnp.dot(x.astype(BF16), ones_blockdiag, preferred_element_type=F32)


def _modulated_norm(x, gain, shift, scale):
    ms = jnp.mean(x * x, axis=-1, keepdims=True)
    return (x * lax.rsqrt(ms + EPS)) * (gain * (1.0 + scale)) + shift


def _ada_kernel(c_ref, w_ref, b_ref, o_ref):
    c = c_ref[...]
    c_act = c * jax.nn.sigmoid(c)
    c_hi, c_lo = _split_bf16(c_act)
    w_hi, w_lo = _split_bf16(w_ref[0])
    acc = jnp.dot(c_hi, w_hi, preferred_element_type=F32)
    acc += jnp.dot(c_lo, w_hi, preferred_element_type=F32)
    acc += jnp.dot(c_hi, w_lo, preferred_element_type=F32)
    o_ref[0] = acc + b_ref[pl.ds(pl.program_id(0), 1), :]


def _ada_call(c, w_ada, b_ada):
    depth, d, n = w_ada.shape
    b = c.shape[0]
    tn = n // 4
    return pl.pallas_call(
        _ada_kernel,
        grid=(depth, n // tn),
        in_specs=[
            pl.BlockSpec((b, d), lambda l, j: (0, 0)),
            pl.BlockSpec((1, d, tn), lambda l, j: (l, 0, j)),
            pl.BlockSpec((depth, tn), lambda l, j: (0, j)),
        ],
        out_specs=pl.BlockSpec((1, b, tn), lambda l, j: (l, 0, j)),
        out_shape=jax.ShapeDtypeStruct((depth, b, n), F32),
        compiler_params=pltpu.CompilerParams(
            dimension_semantics=("arbitrary", "arbitrary"), vmem_limit_bytes=VMEM_LIMIT),
        name="ada_mod",
    )(c, w_ada, b_ada)


def _conformer_pieces(h_scr, ph_scr, cw_ref, cbias_ref, lng_ref, lnb_ref, dst_ref):
    rows, w = dst_ref.shape
    first = HALO - (CONF_KERNEL - 1)
    phases = []
    for phase in range(SUBLANES):
        taps = [k for k in range(CONF_KERNEL) if (first + k) % SUBLANES == phase]
        src = (h_scr, first + taps[0]) if phase == 0 else (ph_scr.at[phase], 0)
        phases.append((taps,) + src)

    def shifted_copies():
        for phase in range(1, SUBLANES):
            taps = phases[phase][0]
            base = first + taps[0]
            span = taps[-1] - taps[0] + rows
            ph_scr[phase, 0:span, :] = h_scr[base:base + span, :]

    def chunk(r, bias=None):
        acc = jnp.broadcast_to(cbias_ref[...] if bias is None else bias, (CONV_CHUNK, w))
        for taps, src, base in phases:
            window = src[r + base:r + base + taps[-1] - taps[0] + CONV_CHUNK, :]
            for k in taps:
                o = k - taps[0]
                acc = acc + cw_ref[k:k + 1, :] * window[o:o + CONV_CHUNK, :]
        mu = jnp.mean(acc, axis=-1, keepdims=True)
        cen = acc - mu
        var = jnp.mean(cen * cen, axis=-1, keepdims=True)
        y = cen * lax.rsqrt(var + EPS) * lng_ref[...] + lnb_ref[...]
        out = y * jax.nn.sigmoid(y)
        dst_ref[r:r + CONV_CHUNK, :] = out.astype(dst_ref.dtype)
        return out[0:ANCHOR_ROWS, 0:LANES]

    return [shifted_copies] + [functools.partial(chunk, r) for r in range(0, rows, CONV_CHUNK)]


def _inproj_mixers_kernel(*refs, layer, n_slabs):
    (x_ref, mod_ref, g_ref, w_ref, qkg_ref, ones32_ref, sw_ref, pw_ref, ps_ref) = refs[:9]
    this_layer = slice(layer, layer + 1)
    wide_refs = refs[9:9 + n_slabs]
    qt_ref, k_ref, vt_ref, glu_ref, yo_ref = refs[9 + n_slabs:14 + n_slabs]
    narrow_refs = refs[14 + n_slabs:14 + 2 * n_slabs]
    hb_scr, pa_scr, pb_scr, c_scr, d_scr, e_scr, f_scr = refs[14 + 2 * n_slabs:]

    for wide, narrow in zip(wide_refs, narrow_refs):
        narrow[...] = wide[0].astype(narrow.dtype)

    ts = x_ref.shape[1]
    w = GROUP_WIDTH
    n_cols = w_ref.shape[1] // w
    rows = SUB_TILE
    j = pl.program_id(1)

    @pl.when(j == 0)
    def _():
        zeros = jnp.zeros((HALO, w), F32)
        c_scr[0:HALO, :] = zeros
        d_scr[0:HALO, :] = zeros

    lane = lax.broadcasted_iota(jnp.int32, (1, w), 1)
    group = lane // (w // len(POOL_WINDOWS))
    col = lambda c: slice(c * w, (c + 1) * w)
    d = x_ref.shape[2]
    batch_row = pl.ds(pl.program_id(0), 1)
    mod = lambda r: mod_ref[0, batch_row, r * d:(r + 1) * d]

    def matmul_pieces(r0, dst):
        def normalise():
            hb_scr[...] = _modulated_norm(x_ref[0, r0:r0 + rows, :], g_ref[this_layer, :],
                                          mod(0), mod(1)).astype(BF16)

        def project(c):
            dst[:, col(c)] = jnp.dot(hb_scr[...], w_ref[:, col(c)], preferred_element_type=F32)

        return [normalise] + [functools.partial(project, c) for c in range(n_cols)]

    def vector_pieces(r0, src):
        tile = slice(r0, r0 + rows)
        scr = slice(HALO + r0, HALO + r0 + rows)

        def qk_norm(c):
            p = src[:, col(c)]
            ms = _group_sum(p * p, ones32_ref[...]) * (1.0 / QK_DIM)
            p = p * lax.rsqrt(ms + EPS) * qkg_ref[this_layer, col(c)]
            if c == 0:
                qt_ref[0, :, tile] = p.T.astype(qt_ref.dtype)
            else:
                k_ref[0, tile, :] = p.astype(k_ref.dtype)

        def v_and_glu():
            vt_ref[0, :, tile] = src[:, col(2)].T.astype(vt_ref.dtype)
            glu_ref[0, tile, :] = src[:, col(3)] * jax.nn.sigmoid(src[:, col(4)])

        def short_conv():
            c_scr[scr, :] = src[:, col(6)] * src[:, col(7)]
            conv = None
            for k in range(SHORT_KERNEL):
                off = HALO + r0 - (SHORT_KERNEL - 1) + k
                term = sw_ref[0, k:k + 1, :] * c_scr[off:off + rows, :]
                conv = term if conv is None else conv + term
            yo_ref[0, tile, col(0)] = (src[:, col(5)] * conv).astype(yo_ref.dtype)

        def pool():
            d_scr[scr, :] = src[:, col(8)]
            end = HALO + r0 + rows
            lo = r0 + SUBLANES
            e_scr[lo:end, :] = d_scr[lo:end, :] + d_scr[lo - 1:end - 1, :]
            pooled = e_scr[scr, :]
            win = jnp.full((1, w), POOL_WINDOWS[0], jnp.int32)
            a, b = e_scr, f_scr
            for g in range(1, len(POOL_WINDOWS)):
                shift = POOL_WINDOWS[g - 1]
                lo = r0 + SUBLANES * (g + 1)
                b[lo:end, :] = a[lo:end, :] + a[lo - shift:end - shift, :]
                pooled = jnp.where(group >= g, b[scr, :], pooled)
                win = jnp.where(group >= g, POOL_WINDOWS[g], win)
                a, b = b, a
            t_pos = j * ts + r0 + lax.broadcasted_iota(jnp.int32, (rows, w), 0)
            cnt = jnp.minimum(t_pos + 1, win).astype(F32)
            y = (pooled / cnt - d_scr[scr, :]).astype(BF16)
            yd = jnp.dot(y, pw_ref[0], preferred_element_type=F32) * ps_ref[this_layer, :]
            yo_ref[0, tile, col(1)] = yd.astype(yo_ref.dtype)

        return [functools.partial(qk_norm, 0), functools.partial(qk_norm, 1), v_and_glu,
                short_conv, pool]

    buffers = (pa_scr, pb_scr)
    n_sub = ts // rows
    for i in range(n_sub + 1):
        mm = matmul_pieces(i * rows, buffers[i % 2]) if i < n_sub else []
        vec = vector_pieces((i - 1) * rows, buffers[(i - 1) % 2]) if i > 0 else []
        stride = max(1, len(mm) // max(1, len(vec)))
        for k in range(max(len(mm), stride * len(vec))):
            if k < len(mm):
                mm[k]()
            if k % stride == stride - 1 and k // stride < len(vec):
                vec[k // stride]()

    for buf in (c_scr, d_scr):
        buf[0:HALO, :] = buf[ts:ts + HALO, :]


def _inproj_mixers_call(layer, x, mods, norm_g, w_in, qk_gain, ones32, short_w, pool_bd, pool_scale,
                        later_weights):
    b, s, d = x.shape
    n = w_in.shape[1]
    ts = INPROJ_TILE
    w = GROUP_WIDTH
    n_tiles = s // ts
    n_steps = b * n_tiles
    per_layer = lambda *shape: pl.BlockSpec((1,) + shape, lambda i, j: (layer,) + (0,) * len(shape))
    whole = lambda arr: pl.BlockSpec(arr.shape, lambda i, j: (0,) * arr.ndim)
    rows_of = lambda width: pl.BlockSpec((1, ts, width), lambda i, j: (i, j, 0))
    transposed = pl.BlockSpec((1, w, ts), lambda i, j: (i, 0, j))
    slab_in, slab_out, slab_shape = [], [], []
    for wide, wide_layer in later_weights:
        _, r, c = wide.shape
        slab_in.append(pl.BlockSpec((1, r // n_steps, c),
                                    functools.partial(lambda wl, i, j: (wl, i * n_tiles + j, 0), wide_layer)))
        slab_out.append(pl.BlockSpec((r // n_steps, c), lambda i, j: (i * n_tiles + j, 0)))
        slab_shape.append(jax.ShapeDtypeStruct((r, c), BF16))
    return pl.pallas_call(
        functools.partial(_inproj_mixers_kernel, layer=layer, n_slabs=len(later_weights)),
        grid=(b, s // ts),
        in_specs=[
            rows_of(d),
            per_layer(b, N_MOD * d),
            whole(norm_g),
            pl.BlockSpec((d, n), lambda i, j: (0, 0)),
            whole(qk_gain),
            pl.BlockSpec((w, w), lambda i, j: (0, 0)),
            per_layer(SHORT_KERNEL, w), per_layer(w, w), whole(pool_scale),
        ] + slab_in,
        out_specs=[transposed, rows_of(w), transposed, rows_of(w), rows_of(2 * w)] + slab_out,
        out_shape=[jax.ShapeDtypeStruct((b, w, s), BF16),
                   jax.ShapeDtypeStruct((b, s, w), BF16),
                   jax.ShapeDtypeStruct((b, w, s), BF16),
                   jax.ShapeDtypeStruct((b, s, w), F32),
                   jax.ShapeDtypeStruct((b, s, 2 * w), BF16)] + slab_shape,
        scratch_shapes=[pltpu.VMEM((SUB_TILE, d), BF16),
                        pltpu.VMEM((SUB_TILE, n), F32), pltpu.VMEM((SUB_TILE, n), F32)]
        + [pltpu.VMEM((HALO + ts, w), F32) for _ in range(4)],
        compiler_params=pltpu.CompilerParams(
            dimension_semantics=("arbitrary", "arbitrary"), vmem_limit_bytes=VMEM_LIMIT),
        name="inproj_mixers",
    )(x, mods, norm_g, w_in, qk_gain, ones32, short_w, pool_bd, pool_scale,
      *[wide for wide, _ in later_weights])


def _attn_kernel(bound_ref, qa_ref, qb_ref, qn_ref, k_ref, vt_ref, lamp_ref, subg_ref, o_ref,
                 vt_scr, qma_scr, qmb_scr, qmn_scr, sa_scr, sb_scr, m_scr, acc_scr, ot_scr,
                 *, layer, lam_init):
    t = qa_ref.shape[2]
    n_chunks = k_ref.shape[1] // t
    step = pl.program_id(1)
    score_bound = bound_ref[layer]
    bound_is_safe = score_bound <= MAX_SAFE_SCORE_BOUND

    @pl.when(step == 0)
    def _():
        for c in range(n_chunks):
            for head in range(ATTN_HEADS):
                r = head * VT_ROWS
                vt_scr[c, r:r + V_DIM, :] = vt_ref[0, head * V_DIM:(head + 1) * V_DIM, c * t:(c + 1) * t]
                vt_scr[c, r + V_DIM:r + VT_ROWS, :] = jnp.ones((VT_ROWS - V_DIM, t), BF16)
        for qm in (qma_scr, qmb_scr, qmn_scr):
            qm[...] = jnp.zeros(qm.shape, BF16)

    for g in range(N_MAPS):
        rows = slice(g * QK_DIM, (g + 1) * QK_DIM)
        qma_scr[g, rows, :] = qa_ref[0, rows, :]
        qmb_scr[g, rows, :] = qb_ref[0, rows, :]
        qmn_scr[g, rows, :] = qn_ref[0, rows, :]

    def fold(x, op):
        parts = [x[r:r + SUBLANES, :] for r in range(0, x.shape[0], SUBLANES)]
        while len(parts) > 1:
            parts = [op(parts[i], parts[i + 1]) for i in range(0, len(parts), 2)]
        return parts[0]

    def stage(bounded, score_jobs, soft_chunk, soft_src, diagonal=False):
        keys = [k_ref[0, pl.ds(pl.multiple_of(job[0] * t, t), t), :] for job in score_jobs]
        if diagonal:
            key = lax.broadcasted_iota(jnp.int32, (t, t), 0)
            query = lax.broadcasted_iota(jnp.int32, (t, t), 1)
            keep = key <= query

        def scores(g):
            for kb, (_, qm, dst) in zip(keys, score_jobs):
                dst[g] = jnp.dot(kb, qm[g], preferred_element_type=F32)

        for g in range(SCORE_LEAD):
            scores(g)
        for g in range(N_MAPS):
            if g + SCORE_LEAD < N_MAPS:
                scores(g + SCORE_LEAD)
            if soft_chunk is None:
                continue
            head = g // 2
            st = soft_src[g]
            if diagonal:
                st = jnp.where(keep, st, -jnp.inf)
            vt = vt_scr[soft_chunk, head * VT_ROWS:(head + 1) * VT_ROWS, :]
            if bounded:
                p = jnp.exp2(st - score_bound).astype(BF16)
                acc = acc_scr[g] + jnp.dot(vt, p, preferred_element_type=F32)
            else:
                m_prev = m_scr[g]
                m_new = jnp.maximum(m_prev, jnp.max(fold(st, jnp.maximum), axis=0, keepdims=True))
                alpha = jnp.exp2(m_prev - m_new)
                p = jnp.exp2((st - m_new).astype(BF16))
                m_scr[g] = m_new
                pv = jnp.dot(vt, p, preferred_element_type=F32)
                acc = acc_scr[g] * alpha + pv
            if not diagonal:
                acc_scr[g] = acc
                continue
            o = acc[0:V_DIM, :] * (1.0 / acc[V_DIM:V_DIM + 1, :])
            if g % 2 == 0:
                o_first = o
            else:
                out = o_first - lam * o
                ms = jnp.mean(out * out, axis=0, keepdims=True)
                out = out * lax.rsqrt(ms + EPS) * subg_ref[0] * (1.0 - lam_init)
                ot_scr[head * V_DIM:(head + 1) * V_DIM, :] = out

    lp = lamp_ref[0]
    lam = (jnp.exp(jnp.sum(lp[0:1] * lp[1:2], axis=-1, keepdims=True))
           - jnp.exp(jnp.sum(lp[2:3] * lp[3:4], axis=-1, keepdims=True)) + lam_init)

    def query_block(odd, qm, qm_next, out_rows):
        qi = 2 * step + odd
        m_scr[...] = jnp.full(m_scr.shape, -jnp.inf, F32)
        acc_scr[...] = jnp.zeros(acc_scr.shape, F32)

        def run(bounded, first, second):
            if not odd:
                @pl.when(step == 0)
                def _():
                    stage(bounded, [(0, qm, first)], None, None)

            def body(i, carry):
                stage(bounded, [(2 * i + 1, qm, second)], 2 * i, first)
                stage(bounded, [(2 * i + 2, qm, first)], 2 * i + 1, second)
                return carry

            lax.fori_loop(0, step, body, 0)
            if odd:
                stage(bounded, [(qi, qm, second)], qi - 1, first)
                stage(bounded, [(0, qm_next, first)], qi, second, diagonal=True)
            else:
                stage(bounded, [(0, qm_next, second)], qi, first, diagonal=True)

        arrives_in_b = (step + odd) % 2
        for bounded in (True, False):
            chosen = bound_is_safe if bounded else jnp.logical_not(bound_is_safe)

            @pl.when(jnp.logical_and(chosen, arrives_in_b == 0))
            def _():
                run(bounded, sa_scr, sb_scr)

            @pl.when(jnp.logical_and(chosen, arrives_in_b == 1))
            def _():
                run(bounded, sb_scr, sa_scr)

        o_ref[0, out_rows, :] = ot_scr[...].T.astype(o_ref.dtype)

    query_block(0, qma_scr, qmb_scr, slice(0, t))
    query_block(1, qmb_scr, qmn_scr, slice(t, 2 * t))


def _attn_call(layer, score_bounds, qt, k, vt, lam_params, sub_gain, lam_init):
    b, s, _ = k.shape
    t = ATTN_TILE
    w = GROUP_WIDTH
    return pl.pallas_call(
        functools.partial(_attn_kernel, layer=layer, lam_init=lam_init),
        grid=(b, s // (2 * t)),
        in_specs=[
            pl.BlockSpec(memory_space=pltpu.SMEM),
            pl.BlockSpec((1, w, t), lambda i, j: (i, 0, 2 * j)),
            pl.BlockSpec((1, w, t), lambda i, j: (i, 0, 2 * j + 1)),
            pl.BlockSpec((1, w, t), lambda i, j: (i, 0, jnp.minimum(2 * j + 2, s // t - 1))),
            pl.BlockSpec((1, s, w), lambda i, j: (i, 0, 0)),
            pl.BlockSpec((1, w, s), lambda i, j: (i, 0, 0)),
            pl.BlockSpec((1, 4, QK_DIM), lambda i, j: (layer, 0, 0)),
            pl.BlockSpec((1, V_DIM, t), lambda i, j: (layer, 0, 0)),
        ],
        out_specs=pl.BlockSpec((1, 2 * t, w), lambda i, j: (i, j, 0)),
        out_shape=jax.ShapeDtypeStruct((b, s, w), BF16),
        scratch_shapes=[
            pltpu.VMEM((s // t, ATTN_HEADS * VT_ROWS, t), BF16),
            pltpu.VMEM((N_MAPS, w, t), BF16),
            pltpu.VMEM((N_MAPS, w, t), BF16),
            pltpu.VMEM((N_MAPS, w, t), BF16),
            pltpu.VMEM((N_MAPS, t, t), F32),
            pltpu.VMEM((N_MAPS, t, t), F32),
            pltpu.VMEM((N_MAPS, 1, t), F32),
            pltpu.VMEM((N_MAPS, VT_ROWS, t), F32),
            pltpu.VMEM((w, t), F32),
        ],
        compiler_params=pltpu.CompilerParams(
            dimension_semantics=("arbitrary", "arbitrary"), vmem_limit_bytes=VMEM_LIMIT),
        name="diff_attn",
    )(score_bounds, qt, qt, qt, k, vt, lam_params, sub_gain)


def _outproj_mlp_kernel(x_ref, ya_ref, yo_ref, glu0_ref, glun_ref, mod_ref, g_ref,
                        wout_ref, w1_ref, w2_ref, cw_ref, cbias_ref, lng_ref, lnb_ref,
                        o_ref, x1_scr, hb_scr, a_scr, yb_scr, hw_scr, ph_scr, *, layer):
    tm = x_ref.shape[1]
    w = GROUP_WIDTH
    d_ff = w1_ref.shape[1]
    j = pl.program_id(1)
    n_tiles = pl.num_programs(1)
    first_step = jnp.logical_and(pl.program_id(0) == 0, j == 0)
    d = x_ref.shape[2]
    batch_row = pl.ds(pl.program_id(0), 1)
    mod = lambda r: mod_ref[0, batch_row, r * d:(r + 1) * d]
    this_layer = pl.ds(layer, 1)

    def conformer_pieces():
        return _conformer_pieces(hw_scr, ph_scr, cw_ref.at[0], cbias_ref.at[this_layer],
                                 lng_ref.at[this_layer], lnb_ref.at[this_layer], yb_scr)

    @pl.when(first_step)
    def _():
        hw_scr[0:HALO, :] = jnp.zeros((HALO, w), F32)
        hw_scr[HALO:, :] = glu0_ref[0]
        for piece in conformer_pieces():
            piece()
        hw_scr[0:HALO, :] = hw_scr[tm:, :]

    mix = jnp.dot(ya_ref[0], wout_ref[0:w, :], preferred_element_type=F32)
    mix = mix + jnp.dot(yb_scr[...], wout_ref[w:2 * w, :], preferred_element_type=F32)
    mix = mix + jnp.dot(yo_ref[0], wout_ref[2 * w:, :], preferred_element_type=F32)
    x1 = x_ref[0] + mod(2) * mix
    x1_scr[...] = x1
    hb_scr[...] = _modulated_norm(x1, g_ref[this_layer, :], mod(3), mod(4)).astype(BF16)

    tail = hw_scr[0:HALO, :]
    hw_scr[0:HALO, :] = jnp.where(j == n_tiles - 1, 0.0, tail)
    hw_scr[HALO:, :] = glun_ref[0]
    pieces = conformer_pieces()
    pieces.pop(0)()

    always = j < n_tiles

    def anchor(ref, cols, slab):
        kept = ref[0:ANCHOR_ROWS, cols]
        ref[0:ANCHOR_ROWS, cols] = jnp.where(always, kept, slab.astype(kept.dtype))

    chunk = 1024
    n_chunks = d_ff // chunk
    per_chunk = len(pieces) // (2 * n_chunks)
    first_lanes = slice(0, LANES)

    def mixer_group(released_by, next_input, next_cols):
        bias = jnp.where(always, cbias_ref[this_layer, :], released_by[0:1, 0:w])
        for _ in range(per_chunk):
            anchor(next_input, next_cols, pieces.pop(0)(bias))

    released_by = x1
    for ci in range(n_chunks):
        c = ci * chunk
        a = jnp.dot(hb_scr[...], w1_ref[:, c:c + chunk], preferred_element_type=F32)
        a_scr[:, c:c + chunk] = jnp.square(jnp.maximum(a, 0.0)).astype(BF16)
        if ci + 1 < n_chunks:
            mixer_group(released_by, hb_scr, first_lanes)
        else:
            mixer_group(released_by, a_scr, first_lanes)
        released_by = a
    y = None
    for ci in range(n_chunks):
        c = ci * chunk
        part = jnp.dot(a_scr[:, c:c + chunk], w2_ref[c:c + chunk, :], preferred_element_type=F32)
        y = part if y is None else y + part
        if ci + 1 < n_chunks:
            mixer_group(released_by, a_scr, slice(c + chunk, c + chunk + LANES))
        else:
            mixer_group(released_by, x1_scr, first_lanes)
        released_by = part
    for piece in pieces:
        piece()
    hw_scr[0:HALO, :] = hw_scr[tm:, :]
    o_ref[0] = x1_scr[...] + mod(5) * y


def _outproj_mlp_call(layer, x, ya, glu, yo, mods, norm_g, w_out, w1, w2, conf_w, conf_b, ln_g, ln_b):
    b, s, d = x.shape
    d_ff = w1.shape[1]
    tm = ROW_TILE
    w = GROUP_WIDTH
    n_tiles = s // tm
    resident = lambda arr: pl.BlockSpec(arr.shape, lambda i, j: (0, 0), pipeline_mode=pl.Buffered(1))
    per_layer = lambda *shape: pl.BlockSpec((1,) + shape, lambda i, j: (layer,) + (0,) * len(shape))
    rows_of = lambda width: pl.BlockSpec((1, tm, width), lambda i, j: (i, j, 0))

    def next_tile(i, j):
        t = jnp.minimum(i * n_tiles + j + 1, b * n_tiles - 1)
        return (t // n_tiles, t % n_tiles, 0)

    whole = lambda arr: pl.BlockSpec(arr.shape, lambda i, j: (0,) * arr.ndim)
    return pl.pallas_call(
        functools.partial(_outproj_mlp_kernel, layer=layer),
        grid=(b, n_tiles),
        in_specs=[
            rows_of(d), rows_of(ya.shape[2]), rows_of(yo.shape[2]),
            pl.BlockSpec((1, tm, w), lambda i, j: (0, 0, 0)),
            pl.BlockSpec((1, tm, w), next_tile),
            per_layer(b, N_MOD * d),
            whole(norm_g),
            resident(w_out), resident(w1), resident(w2),
            per_layer(CONF_KERNEL, w), whole(conf_b), whole(ln_g), whole(ln_b),
        ],
        out_specs=rows_of(d),
        out_shape=jax.ShapeDtypeStruct((b, s, d), F32),
        scratch_shapes=[
            pltpu.VMEM((tm, d), F32),
            pltpu.VMEM((tm, d), BF16),
            pltpu.VMEM((tm, d_ff), BF16),
            pltpu.VMEM((tm, w), BF16),
            pltpu.VMEM((HALO + tm, w), F32),
            pltpu.VMEM((SUBLANES, HALO + tm, w), F32),
        ],
        compiler_params=pltpu.CompilerParams(
            dimension_semantics=("arbitrary", "arbitrary"), vmem_limit_bytes=VMEM_LIMIT),
        name="outproj_mlp",
    )(x, ya, yo, glu, glu, mods, norm_g, w_out, w1, w2, conf_w, conf_b, ln_g, ln_b)


def _block_diag_ones(width, group):
    idx = jnp.arange(width) // group
    return (idx[:, None] == idx[None, :]).astype(BF16)


def kernel(x, c, w_ada, b_ada, norm1_g, norm2_g, w_in, w_out, q_norm_g, k_norm_g, lam_params,
           attn_sub_g, conf_dw_w, conf_dw_b, conf_ln_g, conf_ln_b, short_conv_w, pool_w, pool_scale,
           w_ff1, w_ff2):
    depth, d, _ = w_in.shape
    b = x.shape[0]
    w = GROUP_WIDTH
    mods = _ada_call(c, w_ada, b_ada)
    ones32 = _block_diag_ones(w, QK_DIM)

    qk_gain = jnp.concatenate([jnp.tile(q_norm_g, (1, N_MAPS)) * (QK_DIM ** -0.5 * LOG2_E),
                               jnp.tile(k_norm_g, (1, N_MAPS))], axis=1)
    sub_gain = jnp.broadcast_to(attn_sub_g[:, :, None], (depth, V_DIM, ATTN_TILE))
    score_bounds = (1.02 * QK_DIM * (QK_DIM ** -0.5 * LOG2_E)
                    * jnp.max(jnp.abs(q_norm_g), axis=1) * jnp.max(jnp.abs(k_norm_g), axis=1))
    n_pool = len(POOL_WINDOWS)
    group_eye = jnp.eye(n_pool, dtype=pool_w.dtype)
    pool_bd = jnp.einsum('lgcd,gh->lgchd', pool_w, group_eye).reshape(depth, w, w).astype(BF16)
    w_in_b = w_in[0].astype(BF16)

    for l in range(depth):
        lam_init = 0.8 - 0.6 * math.exp(-0.3 * l)
        later = [(w_out, l), (w_ff1, l), (w_ff2, l)] + ([(w_in, l + 1)] if l + 1 < depth else [])
        qt, k, vt, glu, yo, w_out_b, w_ff1_b, w_ff2_b, *w_in_next = _inproj_mixers_call(
            l, x, mods, norm1_g, w_in_b, qk_gain, ones32,
            short_conv_w, pool_bd, pool_scale, later)
        if w_in_next:
            w_in_b = w_in_next[0]
        ya = _attn_call(l, score_bounds, qt, k, vt, lam_params, sub_gain, lam_init)
        x = _outproj_mlp_call(l, x, ya, glu, yo, mods, norm2_g, w_out_b, w_ff1_b, w_ff2_b,
                              conf_dw_w, conf_dw_b, conf_ln_g, conf_ln_b)
    return x
```

```python
import functools
import math

import jax
import jax.numpy as jnp
from jax import lax
from jax.experimental import pallas as pl
from jax.experimental.pallas import tpu as pltpu

F32 = jnp.float32
BF16 = jnp.bfloat16

EPS = 1e-6
LOG2_E = math.log2(math.e)
N_MOD = 6
ATTN_HEADS = 4
QK_DIM = 32
V_DIM = 64
VT_ROWS = V_DIM + 16
N_MAPS = 2 * ATTN_HEADS
CONF_KERNEL = 31
SHORT_KERNEL = 3
POOL_WINDOWS = (2, 4, 8, 16)
GROUP_WIDTH = 256
SUBLANES = 8
LANES = 128
ANCHOR_ROWS = 16

ROW_TILE = 512
INPROJ_TILE = 1024
SUB_TILE = 256
ATTN_TILE = 256
SCORE_LEAD = 2
MAX_SAFE_SCORE_BOUND = 60.0
CONV_CHUNK = 64
HALO = 32
VMEM_LIMIT = 56 * 1024 * 1024


def _split_bf16(x):
    hi = x.astype(BF16)
    lo = (x - hi.astype(F32)).astype(BF16)
    return hi, lo


def _group_sum(x, ones_blockdiag):
    return jnp.dot(x.astype(BF16), ones_blockdiag, preferred_element_type=F32)


def _modulated_norm(x, gain, shift, scale):
    ms = jnp.mean(x * x, axis=-1, keepdims=True)
    return (x * lax.rsqrt(ms + EPS)) * (gain * (1.0 + scale)) + shift


def _ada_kernel(c_ref, w_ref, b_ref, o_ref):
    c = c_ref[...]
    c_act = c * jax.nn.sigmoid(c)
    c_hi, c_lo = _split_bf16(c_act)
    w_hi, w_lo = _split_bf16(w_ref[0])
    acc = jnp.dot(c_hi, w_hi, preferred_element_type=F32)
    acc += jnp.dot(c_lo, w_hi, preferred_element_type=F32)
    acc += jnp.dot(c_hi, w_lo, preferred_element_type=F32)
    o_ref[0] = acc + b_ref[pl.ds(pl.program_id(0), 1), :]


def _ada_call(c, w_ada, b_ada):
    depth, d, n = w_ada.shape
    b = c.shape[0]
    tn = n // 4
    return pl.pallas_call(
        _ada_kernel,
        grid=(depth, n // tn),
        in_specs=[
            pl.BlockSpec((b, d), lambda l, j: (0, 0)),
            pl.BlockSpec((1, d, tn), lambda l, j: (l, 0, j)),
            pl.BlockSpec((depth, tn), lambda l, j: (0, j)),
        ],
        out_specs=pl.BlockSpec((1, b, tn), lambda l, j: (l, 0, j)),
        out_shape=jax.ShapeDtypeStruct((depth, b, n), F32),
        compiler_params=pltpu.CompilerParams(
            dimension_semantics=("arbitrary", "arbitrary"), vmem_limit_bytes=VMEM_LIMIT),
        name="ada_mod",
    )(c, w_ada, b_ada)


def _conformer_pieces(h_scr, ph_scr, cw_ref, cbias_ref, lng_ref, lnb_ref, dst_ref):
    rows, w = dst_ref.shape
    first = HALO - (CONF_KERNEL - 1)
    phases = []
    for phase in range(SUBLANES):
        taps = [k for k in range(CONF_KERNEL) if (first + k) % SUBLANES == phase]
        src = (h_scr, first + taps[0]) if phase == 0 else (ph_scr.at[phase], 0)
        phases.append((taps,) + src)

    def shifted_copies():
        for phase in range(1, SUBLANES):
            taps = phases[phase][0]
            base = first + taps[0]
            span = taps[-1] - taps[0] + rows
            ph_scr[phase, 0:span, :] = h_scr[base:base + span, :]

    def chunk(r, bias=None):
        acc = jnp.broadcast_to(cbias_ref[...] if bias is None else bias, (CONV_CHUNK, w))
        for taps, src, base in phases:
            window = src[r + base:r + base + taps[-1] - taps[0] + CONV_CHUNK, :]
            for k in taps:
                o = k - taps[0]
                acc = acc + cw_ref[k:k + 1, :] * window[o:o + CONV_CHUNK, :]
        mu = jnp.mean(acc, axis=-1, keepdims=True)
        cen = acc - mu
        var = jnp.mean(cen * cen, axis=-1, keepdims=True)
        y = cen * lax.rsqrt(var + EPS) * lng_ref[...] + lnb_ref[...]
        out = y * jax.nn.sigmoid(y)
        dst_ref[r:r + CONV_CHUNK, :] = out.astype(dst_ref.dtype)
        return out[0:ANCHOR_ROWS, 0:LANES]

    return [shifted_copies] + [functools.partial(chunk, r) for r in range(0, rows, CONV_CHUNK)]


def _inproj_mixers_kernel(*refs, layer, n_slabs):
    (x_ref, mod_ref, g_ref, w_ref, qkg_ref, ones32_ref, sw_ref, pw_ref, ps_ref) = refs[:9]
    this_layer = slice(layer, layer + 1)
    wide_refs = refs[9:9 + n_slabs]
    qt_ref, k_ref, vt_ref, glu_ref, yo_ref = refs[9 + n_slabs:14 + n_slabs]
    narrow_refs = refs[14 + n_slabs:14 + 2 * n_slabs]
    hb_scr, pa_scr, pb_scr, c_scr, d_scr, e_scr, f_scr = refs[14 + 2 * n_slabs:]

    for wide, narrow in zip(wide_refs, narrow_refs):
        narrow[...] = wide[0].astype(narrow.dtype)

    ts = x_ref.shape[1]
    w = GROUP_WIDTH
    n_cols = w_ref.shape[1] // w
    rows = SUB_TILE
    j = pl.program_id(1)

    @pl.when(j == 0)
    def _():
        zeros = jnp.zeros((HALO, w), F32)
        c_scr[0:HALO, :] = zeros
        d_scr[0:HALO, :] = zeros

    lane = lax.broadcasted_iota(jnp.int32, (1, w), 1)
    group = lane // (w // len(POOL_WINDOWS))
    col = lambda c: slice(c * w, (c + 1) * w)
    d = x_ref.shape[2]
    batch_row = pl.ds(pl.program_id(0), 1)
    mod = lambda r: mod_ref[0, batch_row, r * d:(r + 1) * d]

    def matmul_pieces(r0, dst):
        def normalise():
            hb_scr[...] = _modulated_norm(x_ref[0, r0:r0 + rows, :], g_ref[this_layer, :],
                                          mod(0), mod(1)).astype(BF16)

        def project(c):
            dst[:, col(c)] = jnp.dot(hb_scr[...], w_ref[:, col(c)], preferred_element_type=F32)

        return [normalise] + [functools.partial(project, c) for c in range(n_cols)]

    def vector_pieces(r0, src):
        tile = slice(r0, r0 + rows)
        scr = slice(HALO + r0, HALO + r0 + rows)

        def qk_norm(c):
            p = src[:, col(c)]
            ms = _group_sum(p * p, ones32_ref[...]) * (1.0 / QK_DIM)
            p = p * lax.rsqrt(ms + EPS) * qkg_ref[this_layer, col(c)]
            if c == 0:
                qt_ref[0, :, tile] = p.T.astype(qt_ref.dtype)
            else:
                k_ref[0, tile, :] = p.astype(k_ref.dtype)

        def v_and_glu():
            vt_ref[0, :, tile] = src[:, col(2)].T.astype(vt_ref.dtype)
            glu_ref[0, tile, :] = src[:, col(3)] * jax.nn.sigmoid(src[:, col(4)])

        def short_conv():
            c_scr[scr, :] = src[:, col(6)] * src[:, col(7)]
            conv = None
            for k in range(SHORT_KERNEL):
                off = HALO + r0 - (SHORT_KERNEL - 1) + k
                term = sw_ref[0, k:k + 1, :] * c_scr[off:off + rows, :]
                conv = term if conv is None else conv + term
            yo_ref[0, tile, col(0)] = (src[:, col(5)] * conv).astype(yo_ref.dtype)

        def pool():
            d_scr[scr, :] = src[:, col(8)]
            end = HALO + r0 + rows
            lo = r0 + SUBLANES
            e_scr[lo:end, :] = d_scr[lo:end, :] + d_scr[lo - 1:end - 1, :]
            pooled = e_scr[scr, :]
            win = jnp.full((1, w), POOL_WINDOWS[0], jnp.int32)
            a, b = e_scr, f_scr
            for g in range(1, len(POOL_WINDOWS)):
                shift = POOL_WINDOWS[g - 1]
                lo = r0 + SUBLANES * (g + 1)
                b[lo:end, :] = a[lo:end, :] + a[lo - shift:end - shift, :]
                pooled = jnp.where(group >= g, b[scr, :], pooled)
                win = jnp.where(group >= g, POOL_WINDOWS[g], win)
                a, b = b, a
            t_pos = j * ts + r0 + lax.broadcasted_iota(jnp.int32, (rows, w), 0)
            cnt = jnp.minimum(t_pos + 1, win).astype(F32)
            y = (pooled / cnt - d_scr[scr, :]).astype(BF16)
            yd = jnp.dot(y, pw_ref[0], preferred_element_type=F32) * ps_ref[this_layer, :]
            yo_ref[0, tile, col(1)] = yd.astype(yo_ref.dtype)

        return [functools.partial(qk_norm, 0), functools.partial(qk_norm, 1), v_and_glu,
                short_conv, pool]

    buffers = (pa_scr, pb_scr)
    n_sub = ts // rows
    for i in range(n_sub + 1):
        mm = matmul_pieces(i * rows, buffers[i % 2]) if i < n_sub else []
        vec = vector_pieces((i - 1) * rows, buffers[(i - 1) % 2]) if i > 0 else []
        stride = max(1, len(mm) // max(1, len(vec)))
        for k in range(max(len(mm), stride * len(vec))):
            if k < len(mm):
                mm[k]()
            if k % stride == stride - 1 and k // stride < len(vec):
                vec[k // stride]()

    for buf in (c_scr, d_scr):
        buf[0:HALO, :] = buf[ts:ts + HALO, :]


def _inproj_mixers_call(layer, x, mods, norm_g, w_in, qk_gain, ones32, short_w, pool_bd, pool_scale,
                        later_weights):
    b, s, d = x.shape
    n = w_in.shape[1]
    ts = INPROJ_TILE
    w = GROUP_WIDTH
    n_tiles = s // ts
    n_steps = b * n_tiles
    per_layer = lambda *shape: pl.BlockSpec((1,) + shape, lambda i, j: (layer,) + (0,) * len(shape))
    whole = lambda arr: pl.BlockSpec(arr.shape, lambda i, j: (0,) * arr.ndim)
    rows_of = lambda width: pl.BlockSpec((1, ts, width), lambda i, j: (i, j, 0))
    transposed = pl.BlockSpec((1, w, ts), lambda i, j: (i, 0, j))
    slab_in, slab_out, slab_shape = [], [], []
    for wide, wide_layer in later_weights:
        _, r, c = wide.shape
        slab_in.append(pl.BlockSpec((1, r // n_steps, c),
                                    functools.partial(lambda wl, i, j: (wl, i * n_tiles + j, 0), wide_layer)))
        slab_out.append(pl.BlockSpec((r // n_steps, c), lambda i, j: (i * n_tiles + j, 0)))
        slab_shape.append(jax.ShapeDtypeStruct((r, c), BF16))
    return pl.pallas_call(
        functools.partial(_inproj_mixers_kernel, layer=layer, n_slabs=len(later_weights)),
        grid=(b, s // ts),
        in_specs=[
            rows_of(d),
            per_layer(b, N_MOD * d),
            whole(norm_g),
            pl.BlockSpec((d, n), lambda i, j: (0, 0)),
            whole(qk_gain),
            pl.BlockSpec((w, w), lambda i, j: (0, 0)),
            per_layer(SHORT_KERNEL, w), per_layer(w, w), whole(pool_scale),
        ] + slab_in,
        out_specs=[transposed, rows_of(w), transposed, rows_of(w), rows_of(2 * w)] + slab_out,
        out_shape=[jax.ShapeDtypeStruct((b, w, s), BF16),
                   jax.ShapeDtypeStruct((b, s, w), BF16),
                   jax.ShapeDtypeStruct((b, w, s), BF16),
                   jax.ShapeDtypeStruct((b, s, w), F32),
                   jax.ShapeDtypeStruct((b, s, 2 * w), BF16)] + slab_shape,
        scratch_shapes=[pltpu.VMEM((SUB_TILE, d), BF16),
                        pltpu.VMEM((SUB_TILE, n), F32), pltpu.VMEM((SUB_TILE, n), F32)]
        + [pltpu.VMEM((HALO + ts, w), F32) for _ in range(4)],
        compiler_params=pltpu.CompilerParams(
            dimension_semantics=("arbitrary", "arbitrary"), vmem_limit_bytes=VMEM_LIMIT),
        name="inproj_mixers",
    )(x, mods, norm_g, w_in, qk_gain, ones32, short_w, pool_bd, pool_scale,
      *[wide for wide, _ in later_weights])


def _attn_kernel(bound_ref, qa_ref, qb_ref, qn_ref, k_ref, vt_ref, lamp_ref, subg_ref, o_ref,
                 vt_scr, qma_scr, qmb_scr, qmn_scr, sa_scr, sb_scr, m_scr, acc_scr, ot_scr,
                 *, layer, lam_init):
    t = qa_ref.shape[2]
    n_chunks = k_ref.shape[1] // t
    step = pl.program_id(1)
    score_bound = bound_ref[layer]
    bound_is_safe = score_bound <= MAX_SAFE_SCORE_BOUND

    @pl.when(step == 0)
    def _():
        for c in range(n_chunks):
            for head in range(ATTN_HEADS):
                r = head * VT_ROWS
                vt_scr[c, r:r + V_DIM, :] = vt_ref[0, head * V_DIM:(head + 1) * V_DIM, c * t:(c + 1) * t]
                vt_scr[c, r + V_DIM:r + VT_ROWS, :] = jnp.ones((VT_ROWS - V_DIM, t), BF16)
        for qm in (qma_scr, qmb_scr, qmn_scr):
            qm[...] = jnp.zeros(qm.shape, BF16)

    for g in range(N_MAPS):
        rows = slice(g * QK_DIM, (g + 1) * QK_DIM)
        qma_scr[g, rows, :] = qa_ref[0, rows, :]
        qmb_scr[g, rows, :] = qb_ref[0, rows, :]
        qmn_scr[g, rows, :] = qn_ref[0, rows, :]

    def fold(x, op):
        parts = [x[r:r + SUBLANES, :] for r in range(0, x.shape[0], SUBLANES)]
        while len(parts) > 1:
            parts = [op(parts[i], parts[i + 1]) for i in range(0, len(parts), 2)]
        return parts[0]

    def stage(bounded, score_jobs, soft_chunk, soft_src, diagonal=False):
        keys = [k_ref[0, pl.ds(pl.multiple_of(job[0] * t, t), t), :] for job in score_jobs]
        if diagonal:
            key = lax.broadcasted_iota(jnp.int32, (t, t), 0)
            query = lax.broadcasted_iota(jnp.int32, (t, t), 1)
            keep = key <= query

        def scores(g):
            for kb, (_, qm, dst) in zip(keys, score_jobs):
                dst[g] = jnp.dot(kb, qm[g], preferred_element_type=F32)

        for g in range(SCORE_LEAD):
            scores(g)
        for g in range(N_MAPS):
            if g + SCORE_LEAD < N_MAPS:
                scores(g + SCORE_LEAD)
            if soft_chunk is None:
                continue
            head = g // 2
            st = soft_src[g]
            if diagonal:
                st = jnp.where(keep, st, -jnp.inf)
            vt = vt_scr[soft_chunk, head * VT_ROWS:(head + 1) * VT_ROWS, :]
            if bounded:
                p = jnp.exp2(st - score_bound).astype(BF16)
                acc = acc_scr[g] + jnp.dot(vt, p, preferred_element_type=F32)
            else:
                m_prev = m_scr[g]
                m_new = jnp.maximum(m_prev, jnp.max(fold(st, jnp.maximum), axis=0, keepdims=True))
                alpha = jnp.exp2(m_prev - m_new)
                p = jnp.exp2((st - m_new).astype(BF16))
                m_scr[g] = m_new
                pv = jnp.dot(vt, p, preferred_element_type=F32)
                acc = acc_scr[g] * alpha + pv
            if not diagonal:
                acc_scr[g] = acc
                continue
            o = acc[0:V_DIM, :] * (1.0 / acc[V_DIM:V_DIM + 1, :])
            if g % 2 == 0:
                o_first = o
            else:
                out = o_first - lam * o
                ms = jnp.mean(out * out, axis=0, keepdims=True)
                out = out * lax.rsqrt(ms + EPS) * subg_ref[0] * (1.0 - lam_init)
                ot_scr[head * V_DIM:(head + 1) * V_DIM, :] = out

    lp = lamp_ref[0]
    lam = (jnp.exp(jnp.sum(lp[0:1] * lp[1:2], axis=-1, keepdims=True))
           - jnp.exp(jnp.sum(lp[2:3] * lp[3:4], axis=-1, keepdims=True)) + lam_init)

    def query_block(odd, qm, qm_next, out_rows):
        qi = 2 * step + odd
        m_scr[...] = jnp.full(m_scr.shape, -jnp.inf, F32)
        acc_scr[...] = jnp.zeros(acc_scr.shape, F32)

        def run(bounded, first, second):
            if not odd:
                @pl.when(step == 0)
                def _():
                    stage(bounded, [(0, qm, first)], None, None)

            def pair(i):
                stage(bounded, [(2 * i + 1, qm, second)], 2 * i, first)
                stage(bounded, [(2 * i + 2, qm, first)], 2 * i + 1, second)

            if bounded:
                def two_pairs(i, carry):
                    pair(2 * i)
                    pair(2 * i + 1)
                    return carry

                lax.fori_loop(0, step // 2, two_pairs, 0)

                @pl.when(step % 2 == 1)
                def _():
                    pair(step - 1)
            else:
                def one_pair(i, carry):
                    pair(i)
                    return carry

                lax.fori_loop(0, step, one_pair, 0)
            if odd:
                stage(bounded, [(qi, qm, second)], qi - 1, first)
                stage(bounded, [(0, qm_next, first)], qi, second, diagonal=True)
            else:
                stage(bounded, [(0, qm_next, second)], qi, first, diagonal=True)

        arrives_in_b = (step + odd) % 2
        for bounded in (True, False):
            chosen = bound_is_safe if bounded else jnp.logical_not(bound_is_safe)

            @pl.when(jnp.logical_and(chosen, arrives_in_b == 0))
            def _():
                run(bounded, sa_scr, sb_scr)

            @pl.when(jnp.logical_and(chosen, arrives_in_b == 1))
            def _():
                run(bounded, sb_scr, sa_scr)

        o_ref[0, out_rows, :] = ot_scr[...].T.astype(o_ref.dtype)

    query_block(0, qma_scr, qmb_scr, slice(0, t))
    query_block(1, qmb_scr, qmn_scr, slice(t, 2 * t))


def _attn_call(layer, score_bounds, qt, k, vt, lam_params, sub_gain, lam_init):
    b, s, _ = k.shape
    t = ATTN_TILE
    w = GROUP_WIDTH
    return pl.pallas_call(
        functools.partial(_attn_kernel, layer=layer, lam_init=lam_init),
        grid=(b, s // (2 * t)),
        in_specs=[
            pl.BlockSpec(memory_space=pltpu.SMEM),
            pl.BlockSpec((1, w, t), lambda i, j: (i, 0, 2 * j)),
            pl.BlockSpec((1, w, t), lambda i, j: (i, 0, 2 * j + 1)),
            pl.BlockSpec((1, w, t), lambda i, j: (i, 0, jnp.minimum(2 * j + 2, s // t - 1))),
            pl.BlockSpec((1, s, w), lambda i, j: (i, 0, 0)),
            pl.BlockSpec((1, w, s), lambda i, j: (i, 0, 0)),
            pl.BlockSpec((1, 4, QK_DIM), lambda i, j: (layer, 0, 0)),
            pl.BlockSpec((1, V_DIM, t), lambda i, j: (layer, 0, 0)),
        ],
        out_specs=pl.BlockSpec((1, 2 * t, w), lambda i, j: (i, j, 0)),
        out_shape=jax.ShapeDtypeStruct((b, s, w), BF16),
        scratch_shapes=[
            pltpu.VMEM((s // t, ATTN_HEADS * VT_ROWS, t), BF16),
            pltpu.VMEM((N_MAPS, w, t), BF16),
            pltpu.VMEM((N_MAPS, w, t), BF16),
            pltpu.VMEM((N_MAPS, w, t), BF16),
            pltpu.VMEM((N_MAPS, t, t), F32),
            pltpu.VMEM((N_MAPS, t, t), F32),
            pltpu.VMEM((N_MAPS, 1, t), F32),
            pltpu.VMEM((N_MAPS, VT_ROWS, t), F32),
            pltpu.VMEM((w, t), F32),
        ],
        compiler_params=pltpu.CompilerParams(
            dimension_semantics=("arbitrary", "arbitrary"), vmem_limit_bytes=VMEM_LIMIT),
        name="diff_attn",
    )(score_bounds, qt, qt, qt, k, vt, lam_params, sub_gain)


def _outproj_mlp_kernel(x_ref, ya_ref, yo_ref, glu0_ref, glun_ref, mod_ref, g_ref,
                        wout_ref, w1_ref, w2_ref, cw_ref, cbias_ref, lng_ref, lnb_ref,
                        o_ref, x1_scr, hb_scr, a_scr, yb_scr, hw_scr, ph_scr, *, layer):
    tm = x_ref.shape[1]
    w = GROUP_WIDTH
    d_ff = w1_ref.shape[1]
    j = pl.program_id(1)
    n_tiles = pl.num_programs(1)
    first_step = jnp.logical_and(pl.program_id(0) == 0, j == 0)
    d = x_ref.shape[2]
    batch_row = pl.ds(pl.program_id(0), 1)
    mod = lambda r: mod_ref[0, batch_row, r * d:(r + 1) * d]
    this_layer = pl.ds(layer, 1)

    def conformer_pieces():
        return _conformer_pieces(hw_scr, ph_scr, cw_ref.at[0], cbias_ref.at[this_layer],
                                 lng_ref.at[this_layer], lnb_ref.at[this_layer], yb_scr)

    @pl.when(first_step)
    def _():
        hw_scr[0:HALO, :] = jnp.zeros((HALO, w), F32)
        hw_scr[HALO:, :] = glu0_ref[0]
        for piece in conformer_pieces():
            piece()
        hw_scr[0:HALO, :] = hw_scr[tm:, :]

    mix = jnp.dot(ya_ref[0], wout_ref[0:w, :], preferred_element_type=F32)
    mix = mix + jnp.dot(yb_scr[...], wout_ref[w:2 * w, :], preferred_element_type=F32)
    mix = mix + jnp.dot(yo_ref[0], wout_ref[2 * w:, :], preferred_element_type=F32)
    x1 = x_ref[0] + mod(2) * mix
    x1_scr[...] = x1
    hb_scr[...] = _modulated_norm(x1, g_ref[this_layer, :], mod(3), mod(4)).astype(BF16)

    tail = hw_scr[0:HALO, :]
    hw_scr[0:HALO, :] = jnp.where(j == n_tiles - 1, 0.0, tail)
    hw_scr[HALO:, :] = glun_ref[0]
    pieces = conformer_pieces()
    pieces.pop(0)()

    always = j < n_tiles

    def anchor(ref, cols, slab):
        kept = ref[0:ANCHOR_ROWS, cols]
        ref[0:ANCHOR_ROWS, cols] = jnp.where(always, kept, slab.astype(kept.dtype))

    chunk = 1024
    n_chunks = d_ff // chunk
    per_chunk = len(pieces) // (2 * n_chunks)
    first_lanes = slice(0, LANES)

    def mixer_group(released_by, next_input, next_cols):
        bias = jnp.where(always, cbias_ref[this_layer, :], released_by[0:1, 0:w])
        for _ in range(per_chunk):
            anchor(next_input, next_cols, pieces.pop(0)(bias))

    released_by = x1
    for ci in range(n_chunks):
        c = ci * chunk
        a = jnp.dot(hb_scr[...], w1_ref[:, c:c + chunk], preferred_element_type=F32)
        a_scr[:, c:c + chunk] = jnp.square(jnp.maximum(a, 0.0)).astype(BF16)
        if ci + 1 < n_chunks:
            mixer_group(released_by, hb_scr, first_lanes)
        else:
            mixer_group(released_by, a_scr, first_lanes)
        released_by = a
    y = None
    for ci in range(n_chunks):
        c = ci * chunk
        part = jnp.dot(a_scr[:, c:c + chunk], w2_ref[c:c + chunk, :], preferred_element_type=F32)
        y = part if y is None else y + part
        if ci + 1 < n_chunks:
            mixer_group(released_by, a_scr, slice(c + chunk, c + chunk + LANES))
        else:
            mixer_group(released_by, x1_scr, first_lanes)
        released_by = part
    for piece in pieces:
        piece()
    hw_scr[0:HALO, :] = hw_scr[tm:, :]
    o_ref[0] = x1_scr[...] + mod(5) * y


def _outproj_mlp_call(layer, x, ya, glu, yo, mods, norm_g, w_out, w1, w2, conf_w, conf_b, ln_g, ln_b):
    b, s, d = x.shape
    d_ff = w1.shape[1]
    tm = ROW_TILE
    w = GROUP_WIDTH
    n_tiles = s // tm
    resident = lambda arr: pl.BlockSpec(arr.shape, lambda i, j: (0, 0), pipeline_mode=pl.Buffered(1))
    per_layer = lambda *shape: pl.BlockSpec((1,) + shape, lambda i, j: (layer,) + (0,) * len(shape))
    rows_of = lambda width: pl.BlockSpec((1, tm, width), lambda i, j: (i, j, 0))

    def next_tile(i, j):
        t = jnp.minimum(i * n_tiles + j + 1, b * n_tiles - 1)
        return (t // n_tiles, t % n_tiles, 0)

    whole = lambda arr: pl.BlockSpec(arr.shape, lambda i, j: (0,) * arr.ndim)
    return pl.pallas_call(
        functools.partial(_outproj_mlp_kernel, layer=layer),
        grid=(b, n_tiles),
        in_specs=[
            rows_of(d), rows_of(ya.shape[2]), rows_of(yo.shape[2]),
            pl.BlockSpec((1, tm, w), lambda i, j: (0, 0, 0)),
            pl.BlockSpec((1, tm, w), next_tile),
            per_layer(b, N_MOD * d),
            whole(norm_g),
            resident(w_out), resident(w1), resident(w2),
            per_layer(CONF_KERNEL, w), whole(conf_b), whole(ln_g), whole(ln_b),
        ],
        out_specs=rows_of(d),
        out_shape=jax.ShapeDtypeStruct((b, s, d), F32),
        scratch_shapes=[
            pltpu.VMEM((tm, d), F32),
            pltpu.VMEM((tm, d), BF16),
            pltpu.VMEM((tm, d_ff), BF16),
            pltpu.VMEM((tm, w), BF16),
            pltpu.VMEM((HALO + tm, w), F32),
            pltpu.VMEM((SUBLANES, HALO + tm, w), F32),
        ],
        compiler_params=pltpu.CompilerParams(
            dimension_semantics=("arbitrary", "arbitrary"), vmem_limit_bytes=VMEM_LIMIT),
        name="outproj_mlp",
    )(x, ya, yo, glu, glu, mods, norm_g, w_out, w1, w2, conf_w, conf_b, ln_g, ln_b)


def _block_diag_ones(width, group):
    idx = jnp.arange(width) // group
    return (idx[:, None] == idx[None, :]).astype(BF16)


def kernel(x, c, w_ada, b_ada, norm1_g, norm2_g, w_in, w_out, q_norm_g, k_norm_g, lam_params,
           attn_sub_g, conf_dw_w, conf_dw_b, conf_ln_g, conf_ln_b, short_conv_w, pool_w, pool_scale,
           w_ff1, w_ff2):
    depth, d, _ = w_in.shape
    b = x.shape[0]
    w = GROUP_WIDTH
    mods = _ada_call(c, w_ada, b_ada)
    ones32 = _block_diag_ones(w, QK_DIM)

    qk_gain = jnp.concatenate([jnp.tile(q_norm_g, (1, N_MAPS)) * (QK_DIM ** -0.5 * LOG2_E),
                               jnp.tile(k_norm_g, (1, N_MAPS))], axis=1)
    sub_gain = jnp.broadcast_to(attn_sub_g[:, :, None], (depth, V_DIM, ATTN_TILE))
    score_bounds = (1.02 * QK_DIM * (QK_DIM ** -0.5 * LOG2_E)
                    * jnp.max(jnp.abs(q_norm_g), axis=1) * jnp.max(jnp.abs(k_norm_g), axis=1))
    n_pool = len(POOL_WINDOWS)
    group_eye = jnp.eye(n_pool, dtype=pool_w.dtype)
    pool_bd = jnp.einsum('lgcd,gh->lgchd', pool_w, group_eye).reshape(depth, w, w).astype(BF16)
    w_in_b = w_in[0].astype(BF16)

    for l in range(depth):
        lam_init = 0.8 - 0.6 * math.exp(-0.3 * l)
        later = [(w_out, l), (w_ff1, l), (w_ff2, l)] + ([(w_in, l + 1)] if l + 1 < depth else [])
        qt, k, vt, glu, yo, w_out_b, w_ff1_b, w_ff2_b, *w_in_next = _inproj_mixers_call(
            l, x, mods, norm1_g, w_in_b, qk_gain, ones32,
            short_conv_w, pool_bd, pool_scale, later)
        if w_in_next:
            w_in_b = w_in_next[0]
        ya = _attn_call(l, score_bounds, qt, k, vt, lam_params, sub_gain, lam_init)
        x = _outproj_mlp_call(l, x, ya, glu, yo, mods, norm2_g, w_out_b, w_ff1_b, w_ff2_b,
                              conf_dw_w, conf_dw_b, conf_ln_g, conf_ln_b)
    return x
```

```python
import functools
import math

import jax
import jax.numpy as jnp
from jax import lax
from jax.experimental import pallas as pl
from jax.experimental.pallas import tpu as pltpu

F32 = jnp.float32
BF16 = jnp.bfloat16

EPS = 1e-6
LOG2_E = math.log2(math.e)
N_MOD = 6
ATTN_HEADS = 4
QK_DIM = 32
V_DIM = 64
VT_ROWS = V_DIM + 16
N_MAPS = 2 * ATTN_HEADS
CONF_KERNEL = 31
SHORT_KERNEL = 3
POOL_WINDOWS = (2, 4, 8, 16)
GROUP_WIDTH = 256
SUBLANES = 8
LANES = 128
ANCHOR_ROWS = 16

ROW_TILE = 512
INPROJ_TILE = 1024
SUB_TILE = 256
ATTN_TILE = 256
SCORE_LEAD = 2
MAX_SAFE_SCORE_BOUND = 60.0
CONV_CHUNK = 64
HALO = 32
VMEM_LIMIT = 56 * 1024 * 1024


def _split_bf16(x):
    hi = x.astype(BF16)
    lo = (x - hi.astype(F32)).astype(BF16)
    return hi, lo


def _group_sum(x, ones_blockdiag):
    return jnp.dot(x.astype(BF16), ones_blockdiag, preferred_element_type=F32)


def _modulated_norm(x, gain, shift, scale):
    ms = jnp.mean(x * x, axis=-1, keepdims=True)
    return (x * lax.rsqrt(ms + EPS)) * (gain * (1.0 + scale)) + shift


def _ada_kernel(c_ref, w_ref, b_ref, o_ref):
    c = c_ref[...]
    c_act = c * jax.nn.sigmoid(c)
    c_hi, c_lo = _split_bf16(c_act)
    w_hi, w_lo = _split_bf16(w_ref[0])
    acc = jnp.dot(c_hi, w_hi, preferred_element_type=F32)
    acc += jnp.dot(c_lo, w_hi, preferred_element_type=F32)
    acc += jnp.dot(c_hi, w_lo, preferred_element_type=F32)
    o_ref[0] = acc + b_ref[pl.ds(pl.program_id(0), 1), :]


def _ada_call(c, w_ada, b_ada):
    depth, d, n = w_ada.shape
    b = c.shape[0]
    tn = n // 4
    return pl.pallas_call(
        _ada_kernel,
        grid=(depth, n // tn),
        in_specs=[
            pl.BlockSpec((b, d), lambda l, j: (0, 0)),
            pl.BlockSpec((1, d, tn), lambda l, j: (l, 0, j)),
            pl.BlockSpec((depth, tn), lambda l, j: (0, j)),
        ],
        out_specs=pl.BlockSpec((1, b, tn), lambda l, j: (l, 0, j)),
        out_shape=jax.ShapeDtypeStruct((depth, b, n), F32),
        compiler_params=pltpu.CompilerParams(
            dimension_semantics=("arbitrary", "arbitrary"), vmem_limit_bytes=VMEM_LIMIT),
        name="ada_mod",
    )(c, w_ada, b_ada)


def _conformer_pieces(h_scr, ph_scr, cw_ref, cbias_ref, lng_ref, lnb_ref, dst_ref):
    rows, w = dst_ref.shape
    first = HALO - (CONF_KERNEL - 1)
    phases = []
    for phase in range(SUBLANES):
        taps = [k for k in range(CONF_KERNEL) if (first + k) % SUBLANES == phase]
        src = (h_scr, first + taps[0]) if phase == 0 else (ph_scr.at[phase], 0)
        phases.append((taps,) + src)

    def shifted_copies():
        for phase in range(1, SUBLANES):
            taps = phases[phase][0]
            base = first + taps[0]
            span = taps[-1] - taps[0] + rows
            ph_scr[phase, 0:span, :] = h_scr[base:base + span, :]

    def chunk(r, bias=None):
        acc = jnp.broadcast_to(cbias_ref[...] if bias is None else bias, (CONV_CHUNK, w))
        for taps, src, base in phases:
            window = src[r + base:r + base + taps[-1] - taps[0] + CONV_CHUNK, :]
            for k in taps:
                o = k - taps[0]
                acc = acc + cw_ref[k:k + 1, :] * window[o:o + CONV_CHUNK, :]
        mu = jnp.mean(acc, axis=-1, keepdims=True)
        cen = acc - mu
        var = jnp.mean(cen * cen, axis=-1, keepdims=True)
        y = cen * lax.rsqrt(var + EPS) * lng_ref[...] + lnb_ref[...]
        out = y * jax.nn.sigmoid(y)
        dst_ref[r:r + CONV_CHUNK, :] = out.astype(dst_ref.dtype)
        return out[0:ANCHOR_ROWS, 0:LANES]

    return [shifted_copies] + [functools.partial(chunk, r) for r in range(0, rows, CONV_CHUNK)]


def _inproj_mixers_kernel(*refs, layer, n_slabs):
    (x_ref, mod_ref, g_ref, w_ref, qkg_ref, ones32_ref, sw_ref, pw_ref, ps_ref) = refs[:9]
    this_layer = slice(layer, layer + 1)
    wide_refs = refs[9:9 + n_slabs]
    qt_ref, k_ref, vt_ref, glu_ref, yo_ref = refs[9 + n_slabs:14 + n_slabs]
    narrow_refs = refs[14 + n_slabs:14 + 2 * n_slabs]
    hb_scr, pa_scr, pb_scr, c_scr, d_scr, e_scr, f_scr = refs[14 + 2 * n_slabs:]

    for wide, narrow in zip(wide_refs, narrow_refs):
        narrow[...] = wide[0].astype(narrow.dtype)

    ts = x_ref.shape[1]
    w = GROUP_WIDTH
    n_cols = w_ref.shape[1] // w
    rows = SUB_TILE
    j = pl.program_id(1)

    @pl.when(j == 0)
    def _():
        zeros = jnp.zeros((HALO, w), F32)
        c_scr[0:HALO, :] = zeros
        d_scr[0:HALO, :] = zeros

    lane = lax.broadcasted_iota(jnp.int32, (1, w), 1)
    group = lane // (w // len(POOL_WINDOWS))
    col = lambda c: slice(c * w, (c + 1) * w)
    d = x_ref.shape[2]
    batch_row = pl.ds(pl.program_id(0), 1)
    mod = lambda r: mod_ref[0, batch_row, r * d:(r + 1) * d]

    def matmul_pieces(r0, dst):
        def normalise():
            hb_scr[...] = _modulated_norm(x_ref[0, r0:r0 + rows, :], g_ref[this_layer, :],
                                          mod(0), mod(1)).astype(BF16)

        def project(c):
            dst[:, col(c)] = jnp.dot(hb_scr[...], w_ref[:, col(c)], preferred_element_type=F32)

        return [normalise] + [functools.partial(project, c) for c in range(n_cols)]

    def vector_pieces(r0, src):
        tile = slice(r0, r0 + rows)
        scr = slice(HALO + r0, HALO + r0 + rows)

        def qk_norm(c):
            p = src[:, col(c)]
            ms = _group_sum(p * p, ones32_ref[...]) * (1.0 / QK_DIM)
            p = p * lax.rsqrt(ms + EPS) * qkg_ref[this_layer, col(c)]
            if c == 0:
                qt_ref[0, :, tile] = p.T.astype(qt_ref.dtype)
            else:
                k_ref[0, tile, :] = p.astype(k_ref.dtype)

        def v_and_glu():
            vt_ref[0, :, tile] = src[:, col(2)].T.astype(vt_ref.dtype)
            glu_ref[0, tile, :] = src[:, col(3)] * jax.nn.sigmoid(src[:, col(4)])

        def short_conv():
            c_scr[scr, :] = src[:, col(6)] * src[:, col(7)]
            conv = None
            for k in range(SHORT_KERNEL):
                off = HALO + r0 - (SHORT_KERNEL - 1) + k
                term = sw_ref[0, k:k + 1, :] * c_scr[off:off + rows, :]
                conv = term if conv is None else conv + term
            yo_ref[0, tile, col(0)] = (src[:, col(5)] * conv).astype(yo_ref.dtype)

        def pool():
            d_scr[scr, :] = src[:, col(8)]
            end = HALO + r0 + rows
            lo = r0 + SUBLANES
            e_scr[lo:end, :] = d_scr[lo:end, :] + d_scr[lo - 1:end - 1, :]
            pooled = e_scr[scr, :]
            win = jnp.full((1, w), POOL_WINDOWS[0], jnp.int32)
            a, b = e_scr, f_scr
            for g in range(1, len(POOL_WINDOWS)):
                shift = POOL_WINDOWS[g - 1]
                lo = r0 + SUBLANES * (g + 1)
                b[lo:end, :] = a[lo:end, :] + a[lo - shift:end - shift, :]
                pooled = jnp.where(group >= g, b[scr, :], pooled)
                win = jnp.where(group >= g, POOL_WINDOWS[g], win)
                a, b = b, a
            t_pos = j * ts + r0 + lax.broadcasted_iota(jnp.int32, (rows, w), 0)
            cnt = jnp.minimum(t_pos + 1, win).astype(F32)
            y = (pooled / cnt - d_scr[scr, :]).astype(BF16)
            yd = jnp.dot(y, pw_ref[0], preferred_element_type=F32) * ps_ref[this_layer, :]
            yo_ref[0, tile, col(1)] = yd.astype(yo_ref.dtype)

        return [functools.partial(qk_norm, 0), functools.partial(qk_norm, 1), v_and_glu,
                short_conv, pool]

    buffers = (pa_scr, pb_scr)
    n_sub = ts // rows
    for i in range(n_sub + 1):
        mm = matmul_pieces(i * rows, buffers[i % 2]) if i < n_sub else []
        vec = vector_pieces((i - 1) * rows, buffers[(i - 1) % 2]) if i > 0 else []
        stride = max(1, len(mm) // max(1, len(vec)))
        for k in range(max(len(mm), stride * len(vec))):
            if k < len(mm):
                mm[k]()
            if k % stride == stride - 1 and k // stride < len(vec):
                vec[k // stride]()

    for buf in (c_scr, d_scr):
        buf[0:HALO, :] = buf[ts:ts + HALO, :]


def _inproj_mixers_call(layer, x, mods, norm_g, w_in, qk_gain, ones32, short_w, pool_bd, pool_scale,
                        later_weights):
    b, s, d = x.shape
    n = w_in.shape[1]
    ts = INPROJ_TILE
    w = GROUP_WIDTH
    n_tiles = s // ts
    n_steps = b * n_tiles
    per_layer = lambda *shape: pl.BlockSpec((1,) + shape, lambda i, j: (layer,) + (0,) * len(shape))
    whole = lambda arr: pl.BlockSpec(arr.shape, lambda i, j: (0,) * arr.ndim)
    rows_of = lambda width: pl.BlockSpec((1, ts, width), lambda i, j: (i, j, 0))
    transposed = pl.BlockSpec((1, w, ts), lambda i, j: (i, 0, j))
    slab_in, slab_out, slab_shape = [], [], []
    for wide, wide_layer in later_weights:
        _, r, c = wide.shape
        slab_in.append(pl.BlockSpec((1, r // n_steps, c),
                                    functools.partial(lambda wl, i, j: (wl, i * n_tiles + j, 0), wide_layer)))
        slab_out.append(pl.BlockSpec((r // n_steps, c), lambda i, j: (i * n_tiles + j, 0)))
        slab_shape.append(jax.ShapeDtypeStruct((r, c), BF16))
    return pl.pallas_call(
        functools.partial(_inproj_mixers_kernel, layer=layer, n_slabs=len(later_weights)),
        grid=(b, s // ts),
        in_specs=[
            rows_of(d),
            per_layer(b, N_MOD * d),
            whole(norm_g),
            pl.BlockSpec((d, n), lambda i, j: (0, 0)),
            whole(qk_gain),
            pl.BlockSpec((w, w), lambda i, j: (0, 0)),
            per_layer(SHORT_KERNEL, w), per_layer(w, w), whole(pool_scale),
        ] + slab_in,
        out_specs=[transposed, rows_of(w), transposed, rows_of(w), rows_of(2 * w)] + slab_out,
        out_shape=[jax.ShapeDtypeStruct((b, w, s), BF16),
                   jax.ShapeDtypeStruct((b, s, w), BF16),
                   jax.ShapeDtypeStruct((b, w, s), BF16),
                   jax.ShapeDtypeStruct((b, s, w), F32),
                   jax.ShapeDtypeStruct((b, s, 2 * w), BF16)] + slab_shape,
        scratch_shapes=[pltpu.VMEM((SUB_TILE, d), BF16),
                        pltpu.VMEM((SUB_TILE, n), F32), pltpu.VMEM((SUB_TILE, n), F32)]
        + [pltpu.VMEM((HALO + ts, w), F32) for _ in range(4)],
        compiler_params=pltpu.CompilerParams(
            dimension_semantics=("arbitrary", "arbitrary"), vmem_limit_bytes=VMEM_LIMIT),
        name="inproj_mixers",
    )(x, mods, norm_g, w_in, qk_gain, ones32, short_w, pool_bd, pool_scale,
      *[wide for wide, _ in later_weights])


def _attn_kernel(bound_ref, qa_ref, qb_ref, qn_ref, k_ref, vt_ref, lamp_ref, subg_ref, o_ref,
                 vt_scr, qma_scr, qmb_scr, qmn_scr, sa_scr, sb_scr, m_scr, acc_scr, ot_scr,
                 *, layer, lam_init):
    t = qa_ref.shape[2]
    n_chunks = k_ref.shape[1] // t
    step = pl.program_id(1)
    score_bound = bound_ref[layer]
    bound_is_safe = score_bound <= MAX_SAFE_SCORE_BOUND

    @pl.when(step == 0)
    def _():
        for c in range(n_chunks):
            for head in range(ATTN_HEADS):
                r = head * VT_ROWS
                vt_scr[c, r:r + V_DIM, :] = vt_ref[0, head * V_DIM:(head + 1) * V_DIM, c * t:(c + 1) * t]
                vt_scr[c, r + V_DIM:r + VT_ROWS, :] = jnp.ones((VT_ROWS - V_DIM, t), BF16)
        for qm in (qma_scr, qmb_scr, qmn_scr):
            qm[...] = jnp.zeros(qm.shape, BF16)

    for g in range(N_MAPS):
        rows = slice(g * QK_DIM, (g + 1) * QK_DIM)
        qma_scr[g, rows, :] = qa_ref[0, rows, :]
        qmb_scr[g, rows, :] = qb_ref[0, rows, :]
        qmn_scr[g, rows, :] = qn_ref[0, rows, :]

    def fold(x, op):
        parts = [x[r:r + SUBLANES, :] for r in range(0, x.shape[0], SUBLANES)]
        while len(parts) > 1:
            parts = [op(parts[i], parts[i + 1]) for i in range(0, len(parts), 2)]
        return parts[0]

    def stage(bounded, score_jobs, soft_chunk, soft_src, diagonal=False):
        keys = [k_ref[0, pl.ds(pl.multiple_of(job[0] * t, t), t), :] for job in score_jobs]
        if diagonal:
            key = lax.broadcasted_iota(jnp.int32, (t, t), 0)
            query = lax.broadcasted_iota(jnp.int32, (t, t), 1)
            keep = key <= query

        def scores(g):
            for kb, (_, qm, dst) in zip(keys, score_jobs):
                dst[g] = jnp.dot(kb, qm[g], preferred_element_type=F32)

        for g in range(SCORE_LEAD):
            scores(g)
        for g in range(N_MAPS):
            if g + SCORE_LEAD < N_MAPS:
                scores(g + SCORE_LEAD)
            if soft_chunk is None:
                continue
            head = g // 2
            st = soft_src[g]
            if diagonal:
                st = jnp.where(keep, st, -jnp.inf)
            vt = vt_scr[soft_chunk, head * VT_ROWS:(head + 1) * VT_ROWS, :]
            if bounded:
                p = jnp.exp2(st - score_bound).astype(BF16)
                acc = acc_scr[g] + jnp.dot(vt, p, preferred_element_type=F32)
            else:
                m_prev = m_scr[g]
                m_new = jnp.maximum(m_prev, jnp.max(fold(st, jnp.maximum), axis=0, keepdims=True))
                alpha = jnp.exp2(m_prev - m_new)
                p = jnp.exp2((st - m_new).astype(BF16))
                m_scr[g] = m_new
                pv = jnp.dot(vt, p, preferred_element_type=F32)
                acc = acc_scr[g] * alpha + pv
            if not diagonal:
                acc_scr[g] = acc
                continue
            o = acc[0:V_DIM, :] * (1.0 / acc[V_DIM:V_DIM + 1, :])
            if g % 2 == 0:
                o_first = o
            else:
                out = o_first - lam * o
                ms = jnp.mean(out * out, axis=0, keepdims=True)
                out = out * lax.rsqrt(ms + EPS) * subg_ref[0] * (1.0 - lam_init)
                ot_scr[head * V_DIM:(head + 1) * V_DIM, :] = out

    lp = lamp_ref[0]
    lam = (jnp.exp(jnp.sum(lp[0:1] * lp[1:2], axis=-1, keepdims=True))
           - jnp.exp(jnp.sum(lp[2:3] * lp[3:4], axis=-1, keepdims=True)) + lam_init)

    def query_block(odd, qm, qm_next, out_rows):
        qi = 2 * step + odd
        m_scr[...] = jnp.full(m_scr.shape, -jnp.inf, F32)
        acc_scr[...] = jnp.zeros(acc_scr.shape, F32)

        def run(bounded, first, second):
            if not odd:
                @pl.when(step == 0)
                def _():
                    stage(bounded, [(0, qm, first)], None, None)

            def pair(i):
                stage(bounded, [(2 * i + 1, qm, second)], 2 * i, first)
                stage(bounded, [(2 * i + 2, qm, first)], 2 * i + 1, second)

            def last_chunks():
                if odd:
                    stage(bounded, [(qi, qm, second)], qi - 1, first)
                    stage(bounded, [(0, qm_next, first)], qi, second, diagonal=True)
                else:
                    stage(bounded, [(0, qm_next, second)], qi, first, diagonal=True)

            if bounded:
                def two_pairs(i, carry):
                    pair(2 * i)
                    pair(2 * i + 1)
                    return carry

                lax.fori_loop(0, step // 2, two_pairs, 0)

                @pl.when(step % 2 == 1)
                def _():
                    pair(step - 1)
                    last_chunks()

                @pl.when(step % 2 == 0)
                def _():
                    last_chunks()
            else:
                def one_pair(i, carry):
                    pair(i)
                    return carry

                lax.fori_loop(0, step, one_pair, 0)
                last_chunks()

        arrives_in_b = (step + odd) % 2
        for bounded in (True, False):
            chosen = bound_is_safe if bounded else jnp.logical_not(bound_is_safe)

            @pl.when(jnp.logical_and(chosen, arrives_in_b == 0))
            def _():
                run(bounded, sa_scr, sb_scr)

            @pl.when(jnp.logical_and(chosen, arrives_in_b == 1))
            def _():
                run(bounded, sb_scr, sa_scr)

        o_ref[0, out_rows, :] = ot_scr[...].T.astype(o_ref.dtype)

    query_block(0, qma_scr, qmb_scr, slice(0, t))
    query_block(1, qmb_scr, qmn_scr, slice(t, 2 * t))


def _attn_call(layer, score_bounds, qt, k, vt, lam_params, sub_gain, lam_init):
    b, s, _ = k.shape
    t = ATTN_TILE
    w = GROUP_WIDTH
    return pl.pallas_call(
        functools.partial(_attn_kernel, layer=layer, lam_init=lam_init),
        grid=(b, s // (2 * t)),
        in_specs=[
            pl.BlockSpec(memory_space=pltpu.SMEM),
            pl.BlockSpec((1, w, t), lambda i, j: (i, 0, 2 * j)),
            pl.BlockSpec((1, w, t), lambda i, j: (i, 0, 2 * j + 1)),
            pl.BlockSpec((1, w, t), lambda i, j: (i, 0, jnp.minimum(2 * j + 2, s // t - 1))),
            pl.BlockSpec((1, s, w), lambda i, j: (i, 0, 0)),
            pl.BlockSpec((1, w, s), lambda i, j: (i, 0, 0)),
            pl.BlockSpec((1, 4, QK_DIM), lambda i, j: (layer, 0, 0)),
            pl.BlockSpec((1, V_DIM, t), lambda i, j: (layer, 0, 0)),
        ],
        out_specs=pl.BlockSpec((1, 2 * t, w), lambda i, j: (i, j, 0)),
        out_shape=jax.ShapeDtypeStruct((b, s, w), BF16),
        scratch_shapes=[
            pltpu.VMEM((s // t, ATTN_HEADS * VT_ROWS, t), BF16),
            pltpu.VMEM((N_MAPS, w, t), BF16),
            pltpu.VMEM((N_MAPS, w, t), BF16),
            pltpu.VMEM((N_MAPS, w, t), BF16),
            pltpu.VMEM((N_MAPS, t, t), F32),
            pltpu.VMEM((N_MAPS, t, t), F32),
            pltpu.VMEM((N_MAPS, 1, t), F32),
            pltpu.VMEM((N_MAPS, VT_ROWS, t), F32),
            pltpu.VMEM((w, t), F32),
        ],
        compiler_params=pltpu.CompilerParams(
            dimension_semantics=("arbitrary", "arbitrary"), vmem_limit_bytes=VMEM_LIMIT),
        name="diff_attn",
    )(score_bounds, qt, qt, qt, k, vt, lam_params, sub_gain)


def _outproj_mlp_kernel(x_ref, ya_ref, yo_ref, glu0_ref, glun_ref, mod_ref, g_ref,
                        wout_ref, w1_ref, w2_ref, cw_ref, cbias_ref, lng_ref, lnb_ref,
                        o_ref, x1_scr, hb_scr, a_scr, yb_scr, hw_scr, ph_scr, *, layer):
    tm = x_ref.shape[1]
    w = GROUP_WIDTH
    d_ff = w1_ref.shape[1]
    j = pl.program_id(1)
    n_tiles = pl.num_programs(1)
    first_step = jnp.logical_and(pl.program_id(0) == 0, j == 0)
    d = x_ref.shape[2]
    batch_row = pl.ds(pl.program_id(0), 1)
    mod = lambda r: mod_ref[0, batch_row, r * d:(r + 1) * d]
    this_layer = pl.ds(layer, 1)

    def conformer_pieces():
        return _conformer_pieces(hw_scr, ph_scr, cw_ref.at[0], cbias_ref.at[this_layer],
                                 lng_ref.at[this_layer], lnb_ref.at[this_layer], yb_scr)

    @pl.when(first_step)
    def _():
        hw_scr[0:HALO, :] = jnp.zeros((HALO, w), F32)
        hw_scr[HALO:, :] = glu0_ref[0]
        for piece in conformer_pieces():
            piece()
        hw_scr[0:HALO, :] = hw_scr[tm:, :]

    mix = jnp.dot(ya_ref[0], wout_ref[0:w, :], preferred_element_type=F32)
    mix = mix + jnp.dot(yb_scr[...], wout_ref[w:2 * w, :], preferred_element_type=F32)
    mix = mix + jnp.dot(yo_ref[0], wout_ref[2 * w:, :], preferred_element_type=F32)
    x1 = x_ref[0] + mod(2) * mix
    x1_scr[...] = x1
    hb_scr[...] = _modulated_norm(x1, g_ref[this_layer, :], mod(3), mod(4)).astype(BF16)

    tail = hw_scr[0:HALO, :]
    hw_scr[0:HALO, :] = jnp.where(j == n_tiles - 1, 0.0, tail)
    hw_scr[HALO:, :] = glun_ref[0]
    pieces = conformer_pieces()
    pieces.pop(0)()

    always = j < n_tiles

    def anchor(ref, cols, slab):
        kept = ref[0:ANCHOR_ROWS, cols]
        ref[0:ANCHOR_ROWS, cols] = jnp.where(always, kept, slab.astype(kept.dtype))

    chunk = 1024
    n_chunks = d_ff // chunk
    per_chunk = len(pieces) // (2 * n_chunks)
    first_lanes = slice(0, LANES)

    def mixer_group(released_by, next_input, next_cols):
        bias = jnp.where(always, cbias_ref[this_layer, :], released_by[0:1, 0:w])
        for _ in range(per_chunk):
            anchor(next_input, next_cols, pieces.pop(0)(bias))

    released_by = x1
    for ci in range(n_chunks):
        c = ci * chunk
        a = jnp.dot(hb_scr[...], w1_ref[:, c:c + chunk], preferred_element_type=F32)
        a_scr[:, c:c + chunk] = jnp.square(jnp.maximum(a, 0.0)).astype(BF16)
        if ci + 1 < n_chunks:
            mixer_group(released_by, hb_scr, first_lanes)
        else:
            mixer_group(released_by, a_scr, first_lanes)
        released_by = a
    y = None
    for ci in range(n_chunks):
        c = ci * chunk
        part = jnp.dot(a_scr[:, c:c + chunk], w2_ref[c:c + chunk, :], preferred_element_type=F32)
        y = part if y is None else y + part
        if ci + 1 < n_chunks:
            mixer_group(released_by, a_scr, slice(c + chunk, c + chunk + LANES))
        else:
            mixer_group(released_by, x1_scr, first_lanes)
        released_by = part
    for piece in pieces:
        piece()
    hw_scr[0:HALO, :] = hw_scr[tm:, :]
    o_ref[0] = x1_scr[...] + mod(5) * y


def _outproj_mlp_call(layer, x, ya, glu, yo, mods, norm_g, w_out, w1, w2, conf_w, conf_b, ln_g, ln_b):
    b, s, d = x.shape
    d_ff = w1.shape[1]
    tm = ROW_TILE
    w = GROUP_WIDTH
    n_tiles = s // tm
    resident = lambda arr: pl.BlockSpec(arr.shape, lambda i, j: (0, 0), pipeline_mode=pl.Buffered(1))
    per_layer = lambda *shape: pl.BlockSpec((1,) + shape, lambda i, j: (layer,) + (0,) * len(shape))
    rows_of = lambda width: pl.BlockSpec((1, tm, width), lambda i, j: (i, j, 0))

    def next_tile(i, j):
        t = jnp.minimum(i * n_tiles + j + 1, b * n_tiles - 1)
        return (t // n_tiles, t % n_tiles, 0)

    whole = lambda arr: pl.BlockSpec(arr.shape, lambda i, j: (0,) * arr.ndim)
    return pl.pallas_call(
        functools.partial(_outproj_mlp_kernel, layer=layer),
        grid=(b, n_tiles),
        in_specs=[
            rows_of(d), rows_of(ya.shape[2]), rows_of(yo.shape[2]),
            pl.BlockSpec((1, tm, w), lambda i, j: (0, 0, 0)),
            pl.BlockSpec((1, tm, w), next_tile),
            per_layer(b, N_MOD * d),
            whole(norm_g),
            resident(w_out), resident(w1), resident(w2),
            per_layer(CONF_KERNEL, w), whole(conf_b), whole(ln_g), whole(ln_b),
        ],
        out_specs=rows_of(d),
        out_shape=jax.ShapeDtypeStruct((b, s, d), F32),
        scratch_shapes=[
            pltpu.VMEM((tm, d), F32),
            pltpu.VMEM((tm, d), BF16),
            pltpu.VMEM((tm, d_ff), BF16),
            pltpu.VMEM((tm, w), BF16),
            pltpu.VMEM((HALO + tm, w), F32),
            pltpu.VMEM((SUBLANES, HALO + tm, w), F32),
        ],
        compiler_params=pltpu.CompilerParams(
            dimension_semantics=("arbitrary", "arbitrary"), vmem_limit_bytes=VMEM_LIMIT),
        name="outproj_mlp",
    )(x, ya, yo, glu, glu, mods, norm_g, w_out, w1, w2, conf_w, conf_b, ln_g, ln_b)


def _block_diag_ones(width, group):
    idx = jnp.arange(width) // group
    return (idx[:, None] == idx[None, :]).astype(BF16)


def kernel(x, c, w_ada, b_ada, norm1_g, norm2_g, w_in, w_out, q_norm_g, k_norm_g, lam_params,
           attn_sub_g, conf_dw_w, conf_dw_b, conf_ln_g, conf_ln_b, short_conv_w, pool_w, pool_scale,
           w_ff1, w_ff2):
    depth, d, _ = w_in.shape
    b = x.shape[0]
    w = GROUP_WIDTH
    mods = _ada_call(c, w_ada, b_ada)
    ones32 = _block_diag_ones(w, QK_DIM)

    qk_gain = jnp.concatenate([jnp.tile(q_norm_g, (1, N_MAPS)) * (QK_DIM ** -0.5 * LOG2_E),
                               jnp.tile(k_norm_g, (1, N_MAPS))], axis=1)
    sub_gain = jnp.broadcast_to(attn_sub_g[:, :, None], (depth, V_DIM, ATTN_TILE))
    score_bounds = (1.02 * QK_DIM * (QK_DIM ** -0.5 * LOG2_E)
                    * jnp.max(jnp.abs(q_norm_g), axis=1) * jnp.max(jnp.abs(k_norm_g), axis=1))
    n_pool = len(POOL_WINDOWS)
    group_eye = jnp.eye(n_pool, dtype=pool_w.dtype)
    pool_bd = jnp.einsum('lgcd,gh->lgchd', pool_w, group_eye).reshape(depth, w, w).astype(BF16)
    w_in_b = w_in[0].astype(BF16)

    for l in range(depth):
        lam_init = 0.8 - 0.6 * math.exp(-0.3 * l)
        later = [(w_out, l), (w_ff1, l), (w_ff2, l)] + ([(w_in, l + 1)] if l + 1 < depth else [])
        qt, k, vt, glu, yo, w_out_b, w_ff1_b, w_ff2_b, *w_in_next = _inproj_mixers_call(
            l, x, mods, norm1_g, w_in_b, qk_gain, ones32,
            short_conv_w, pool_bd, pool_scale, later)
        if w_in_next:
            w_in_b = w_in_next[0]
        ya = _attn_call(l, score_bounds, qt, k, vt, lam_params, sub_gain, lam_init)
        x = _outproj_mlp_call(l, x, ya, glu, yo, mods, norm2_g, w_out_b, w_ff1_b, w_ff2_b,
                              conf_dw_w, conf_dw_b, conf_ln_g, conf_ln_b)
    return x
```

```python
import functools
import math

import jax
import jax.numpy as jnp
from jax import lax
from jax.experimental import pallas as pl
from jax.experimental.pallas import tpu as pltpu

F32 = jnp.float32
BF16 = jnp.bfloat16

EPS = 1e-6
LOG2_E = math.log2(math.e)
N_MOD = 6
ATTN_HEADS = 4
QK_DIM = 32
V_DIM = 64
VT_ROWS = V_DIM + 16
N_MAPS = 2 * ATTN_HEADS
CONF_KERNEL = 31
SHORT_KERNEL = 3
POOL_WINDOWS = (2, 4, 8, 16)
GROUP_WIDTH = 256
SUBLANES = 8
LANES = 128
ANCHOR_ROWS = 16

ROW_TILE = 512
INPROJ_TILE = 1024
SUB_TILE = 256
ATTN_TILE = 256
SCORE_LEAD = 2
MAX_SAFE_SCORE_BOUND = 60.0
CONV_CHUNK = 64
HALO = 32
VMEM_LIMIT = 56 * 1024 * 1024


def _split_bf16(x):
    hi = x.astype(BF16)
    lo = (x - hi.astype(F32)).astype(BF16)
    return hi, lo


def _group_sum(x, ones_blockdiag):
    return jnp.dot(x.astype(BF16), ones_blockdiag, preferred_element_type=F32)


def _modulated_norm(x, gain, shift, scale):
    ms = jnp.mean(x * x, axis=-1, keepdims=True)
    return (x * lax.rsqrt(ms + EPS)) * (gain * (1.0 + scale)) + shift


def _ada_kernel(c_ref, w_ref, b_ref, o_ref):
    c = c_ref[...]
    c_act = c * jax.nn.sigmoid(c)
    c_hi, c_lo = _split_bf16(c_act)
    w_hi, w_lo = _split_bf16(w_ref[0])
    acc = jnp.dot(c_hi, w_hi, preferred_element_type=F32)
    acc += jnp.dot(c_lo, w_hi, preferred_element_type=F32)
    acc += jnp.dot(c_hi, w_lo, preferred_element_type=F32)
    o_ref[0] = acc + b_ref[pl.ds(pl.program_id(0), 1), :]


def _ada_call(c, w_ada, b_ada):
    depth, d, n = w_ada.shape
    b = c.shape[0]
    tn = n // 4
    return pl.pallas_call(
        _ada_kernel,
        grid=(depth, n // tn),
        in_specs=[
            pl.BlockSpec((b, d), lambda l, j: (0, 0)),
            pl.BlockSpec((1, d, tn), lambda l, j: (l, 0, j)),
            pl.BlockSpec((depth, tn), lambda l, j: (0, j)),
        ],
        out_specs=pl.BlockSpec((1, b, tn), lambda l, j: (l, 0, j)),
        out_shape=jax.ShapeDtypeStruct((depth, b, n), F32),
        compiler_params=pltpu.CompilerParams(
            dimension_semantics=("arbitrary", "arbitrary"), vmem_limit_bytes=VMEM_LIMIT),
        name="ada_mod",
    )(c, w_ada, b_ada)


def _conformer_pieces(h_scr, ph_scr, cw_ref, cbias_ref, lng_ref, lnb_ref, dst_ref):
    rows, w = dst_ref.shape
    first = HALO - (CONF_KERNEL - 1)
    phases = []
    for phase in range(SUBLANES):
        taps = [k for k in range(CONF_KERNEL) if (first + k) % SUBLANES == phase]
        src = (h_scr, first + taps[0]) if phase == 0 else (ph_scr.at[phase], 0)
        phases.append((taps,) + src)

    def shifted_copies():
        for phase in range(1, SUBLANES):
            taps = phases[phase][0]
            base = first + taps[0]
            span = taps[-1] - taps[0] + rows
            ph_scr[phase, 0:span, :] = h_scr[base:base + span, :]

    def chunk(r, bias=None):
        acc = jnp.broadcast_to(cbias_ref[...] if bias is None else bias, (CONV_CHUNK, w))
        for taps, src, base in phases:
            window = src[r + base:r + base + taps[-1] - taps[0] + CONV_CHUNK, :]
            for k in taps:
                o = k - taps[0]
                acc = acc + cw_ref[k:k + 1, :] * window[o:o + CONV_CHUNK, :]
        mu = jnp.mean(acc, axis=-1, keepdims=True)
        cen = acc - mu
        var = jnp.mean(cen * cen, axis=-1, keepdims=True)
        y = cen * lax.rsqrt(var + EPS) * lng_ref[...] + lnb_ref[...]
        out = y * jax.nn.sigmoid(y)
        dst_ref[r:r + CONV_CHUNK, :] = out.astype(dst_ref.dtype)
        return out[0:ANCHOR_ROWS, 0:LANES]

    return [shifted_copies] + [functools.partial(chunk, r) for r in range(0, rows, CONV_CHUNK)]


def _inproj_mixers_kernel(*refs, layer, n_slabs):
    (x_ref, mod_ref, g_ref, w_ref, qkg_ref, ones32_ref, sw_ref, pw_ref, ps_ref) = refs[:9]
    this_layer = slice(layer, layer + 1)
    wide_refs = refs[9:9 + n_slabs]
    qt_ref, k_ref, vt_ref, glu_ref, yo_ref = refs[9 + n_slabs:14 + n_slabs]
    narrow_refs = refs[14 + n_slabs:14 + 2 * n_slabs]
    hb_scr, pa_scr, pb_scr, c_scr, d_scr, e_scr, f_scr = refs[14 + 2 * n_slabs:]

    for wide, narrow in zip(wide_refs, narrow_refs):
        narrow[...] = wide[0].astype(narrow.dtype)

    ts = x_ref.shape[1]
    w = GROUP_WIDTH
    n_cols = w_ref.shape[1] // w
    rows = SUB_TILE
    j = pl.program_id(1)

    @pl.when(j == 0)
    def _():
        zeros = jnp.zeros((HALO, w), F32)
        c_scr[0:HALO, :] = zeros
        d_scr[0:HALO, :] = zeros

    lane = lax.broadcasted_iota(jnp.int32, (1, w), 1)
    group = lane // (w // len(POOL_WINDOWS))
    col = lambda c: slice(c * w, (c + 1) * w)
    d = x_ref.shape[2]
    batch_row = pl.ds(pl.program_id(0), 1)
    mod = lambda r: mod_ref[0, batch_row, r * d:(r + 1) * d]

    def matmul_pieces(r0, dst):
        def normalise():
            hb_scr[...] = _modulated_norm(x_ref[0, r0:r0 + rows, :], g_ref[this_layer, :],
                                          mod(0), mod(1)).astype(BF16)

        def project(c):
            dst[:, col(c)] = jnp.dot(hb_scr[...], w_ref[:, col(c)], preferred_element_type=F32)

        return [normalise] + [functools.partial(project, c) for c in range(n_cols)]

    def vector_pieces(r0, src):
        tile = slice(r0, r0 + rows)
        scr = slice(HALO + r0, HALO + r0 + rows)

        def qk_norm(c):
            p = src[:, col(c)]
            ms = _group_sum(p * p, ones32_ref[...]) * (1.0 / QK_DIM)
            p = p * lax.rsqrt(ms + EPS) * qkg_ref[this_layer, col(c)]
            if c == 0:
                qt_ref[0, :, tile] = p.T.astype(qt_ref.dtype)
            else:
                k_ref[0, tile, :] = p.astype(k_ref.dtype)

        def v_and_glu():
            vt_ref[0, :, tile] = src[:, col(2)].T.astype(vt_ref.dtype)
            glu_ref[0, tile, :] = src[:, col(3)] * jax.nn.sigmoid(src[:, col(4)])

        def short_conv():
            c_scr[scr, :] = src[:, col(6)] * src[:, col(7)]
            conv = None
            for k in range(SHORT_KERNEL):
                off = HALO + r0 - (SHORT_KERNEL - 1) + k
                term = sw_ref[0, k:k + 1, :] * c_scr[off:off + rows, :]
                conv = term if conv is None else conv + term
            yo_ref[0, tile, col(0)] = (src[:, col(5)] * conv).astype(yo_ref.dtype)

        def pool():
            d_scr[scr, :] = src[:, col(8)]
            end = HALO + r0 + rows
            lo = r0 + SUBLANES
            e_scr[lo:end, :] = d_scr[lo:end, :] + d_scr[lo - 1:end - 1, :]
            pooled = e_scr[scr, :]
            win = jnp.full((1, w), POOL_WINDOWS[0], jnp.int32)
            a, b = e_scr, f_scr
            for g in range(1, len(POOL_WINDOWS)):
                shift = POOL_WINDOWS[g - 1]
                lo = r0 + SUBLANES * (g + 1)
                b[lo:end, :] = a[lo:end, :] + a[lo - shift:end - shift, :]
                pooled = jnp.where(group >= g, b[scr, :], pooled)
                win = jnp.where(group >= g, POOL_WINDOWS[g], win)
                a, b = b, a
            t_pos = j * ts + r0 + lax.broadcasted_iota(jnp.int32, (rows, w), 0)
            cnt = jnp.minimum(t_pos + 1, win).astype(F32)
            y = (pooled / cnt - d_scr[scr, :]).astype(BF16)
            yd = jnp.dot(y, pw_ref[0], preferred_element_type=F32) * ps_ref[this_layer, :]
            yo_ref[0, tile, col(1)] = yd.astype(yo_ref.dtype)

        return [functools.partial(qk_norm, 0), functools.partial(qk_norm, 1), v_and_glu,
                short_conv, pool]

    buffers = (pa_scr, pb_scr)
    n_sub = ts // rows
    for i in range(n_sub + 1):
        mm = matmul_pieces(i * rows, buffers[i % 2]) if i < n_sub else []
        vec = vector_pieces((i - 1) * rows, buffers[(i - 1) % 2]) if i > 0 else []
        stride = max(1, len(mm) // max(1, len(vec)))
        for k in range(max(len(mm), stride * len(vec))):
            if k < len(mm):
                mm[k]()
            if k % stride == stride - 1 and k // stride < len(vec):
                vec[k // stride]()

    for buf in (c_scr, d_scr):
        buf[0:HALO, :] = buf[ts:ts + HALO, :]


def _inproj_mixers_call(layer, x, mods, norm_g, w_in, qk_gain, ones32, short_w, pool_bd, pool_scale,
                        later_weights):
    b, s, d = x.shape
    n = w_in.shape[1]
    ts = INPROJ_TILE
    w = GROUP_WIDTH
    n_tiles = s // ts
    n_steps = b * n_tiles
    per_layer = lambda *shape: pl.BlockSpec((1,) + shape, lambda i, j: (layer,) + (0,) * len(shape))
    whole = lambda arr: pl.BlockSpec(arr.shape, lambda i, j: (0,) * arr.ndim)
    rows_of = lambda width: pl.BlockSpec((1, ts, width), lambda i, j: (i, j, 0))
    transposed = pl.BlockSpec((1, w, ts), lambda i, j: (i, 0, j))
    slab_in, slab_out, slab_shape = [], [], []
    for wide, wide_layer in later_weights:
        _, r, c = wide.shape
        slab_in.append(pl.BlockSpec((1, r // n_steps, c),
                                    functools.partial(lambda wl, i, j: (wl, i * n_tiles + j, 0), wide_layer)))
        slab_out.append(pl.BlockSpec((r // n_steps, c), lambda i, j: (i * n_tiles + j, 0)))
        slab_shape.append(jax.ShapeDtypeStruct((r, c), BF16))
    return pl.pallas_call(
        functools.partial(_inproj_mixers_kernel, layer=layer, n_slabs=len(later_weights)),
        grid=(b, s // ts),
        in_specs=[
            rows_of(d),
            per_layer(b, N_MOD * d),
            whole(norm_g),
            pl.BlockSpec((d, n), lambda i, j: (0, 0)),
            whole(qk_gain),
            pl.BlockSpec((w, w), lambda i, j: (0, 0)),
            per_layer(SHORT_KERNEL, w), per_layer(w, w), whole(pool_scale),
        ] + slab_in,
        out_specs=[transposed, rows_of(w), transposed, rows_of(w), rows_of(2 * w)] + slab_out,
        out_shape=[jax.ShapeDtypeStruct((b, w, s), BF16),
                   jax.ShapeDtypeStruct((b, s, w), BF16),
                   jax.ShapeDtypeStruct((b, w, s), BF16),
                   jax.ShapeDtypeStruct((b, s, w), F32),
                   jax.ShapeDtypeStruct((b, s, 2 * w), BF16)] + slab_shape,
        scratch_shapes=[pltpu.VMEM((SUB_TILE, d), BF16),
                        pltpu.VMEM((SUB_TILE, n), F32), pltpu.VMEM((SUB_TILE, n), F32)]
        + [pltpu.VMEM((HALO + ts, w), F32) for _ in range(4)],
        compiler_params=pltpu.CompilerParams(
            dimension_semantics=("arbitrary", "arbitrary"), vmem_limit_bytes=VMEM_LIMIT),
        name="inproj_mixers",
    )(x, mods, norm_g, w_in, qk_gain, ones32, short_w, pool_bd, pool_scale,
      *[wide for wide, _ in later_weights])


def _attn_kernel(bound_ref, qa_ref, qb_ref, qn_ref, k_ref, vt_ref, lamp_ref, subg_ref, o_ref,
                 vt_scr, qma_scr, qmb_scr, qmn_scr, s_scr, m_scr, acc_scr, ot_scr,
                 *, layer, lam_init):
    t = qa_ref.shape[2]
    n_chunks = k_ref.shape[1] // t
    step = pl.program_id(1)
    score_bound = bound_ref[layer]
    bound_is_safe = score_bound <= MAX_SAFE_SCORE_BOUND

    @pl.when(step == 0)
    def _():
        for c in range(n_chunks):
            for head in range(ATTN_HEADS):
                r = head * VT_ROWS
                vt_scr[c, r:r + V_DIM, :] = vt_ref[0, head * V_DIM:(head + 1) * V_DIM, c * t:(c + 1) * t]
                vt_scr[c, r + V_DIM:r + VT_ROWS, :] = jnp.ones((VT_ROWS - V_DIM, t), BF16)
        for qm in (qma_scr, qmb_scr, qmn_scr):
            qm[...] = jnp.zeros(qm.shape, BF16)

    for g in range(N_MAPS):
        rows = slice(g * QK_DIM, (g + 1) * QK_DIM)
        qma_scr[g, rows, :] = qa_ref[0, rows, :]
        qmb_scr[g, rows, :] = qb_ref[0, rows, :]
        qmn_scr[g, rows, :] = qn_ref[0, rows, :]

    def fold(x, op):
        parts = [x[r:r + SUBLANES, :] for r in range(0, x.shape[0], SUBLANES)]
        while len(parts) > 1:
            parts = [op(parts[i], parts[i + 1]) for i in range(0, len(parts), 2)]
        return parts[0]

    def stage(bounded, score_jobs, soft_chunk, soft_src, diagonal=False):
        keys = [k_ref[0, pl.ds(pl.multiple_of(job[0] * t, t), t), :] for job in score_jobs]
        if diagonal:
            key = lax.broadcasted_iota(jnp.int32, (t, t), 0)
            query = lax.broadcasted_iota(jnp.int32, (t, t), 1)
            keep = key <= query

        def scores(g):
            for kb, (_, qm, dst) in zip(keys, score_jobs):
                dst[g] = jnp.dot(kb, qm[g], preferred_element_type=F32)

        for g in range(SCORE_LEAD):
            scores(g)
        for g in range(N_MAPS):
            if g + SCORE_LEAD < N_MAPS:
                scores(g + SCORE_LEAD)
            if soft_chunk is None:
                continue
            head = g // 2
            st = soft_src[g]
            if diagonal:
                st = jnp.where(keep, st, -jnp.inf)
            vt = vt_scr[soft_chunk, head * VT_ROWS:(head + 1) * VT_ROWS, :]
            if bounded:
                p = jnp.exp2(st - score_bound).astype(BF16)
                acc = acc_scr[g] + jnp.dot(vt, p, preferred_element_type=F32)
            else:
                m_prev = m_scr[g]
                m_new = jnp.maximum(m_prev, jnp.max(fold(st, jnp.maximum), axis=0, keepdims=True))
                alpha = jnp.exp2(m_prev - m_new)
                p = jnp.exp2((st - m_new).astype(BF16))
                m_scr[g] = m_new
                pv = jnp.dot(vt, p, preferred_element_type=F32)
                acc = acc_scr[g] * alpha + pv
            if not diagonal:
                acc_scr[g] = acc
                continue
            o = acc[0:V_DIM, :] * (1.0 / acc[V_DIM:V_DIM + 1, :])
            if g % 2 == 0:
                o_first = o
            else:
                out = o_first - lam * o
                ms = jnp.mean(out * out, axis=0, keepdims=True)
                out = out * lax.rsqrt(ms + EPS) * subg_ref[0] * (1.0 - lam_init)
                ot_scr[head * V_DIM:(head + 1) * V_DIM, :] = out

    lp = lamp_ref[0]
    lam = (jnp.exp(jnp.sum(lp[0:1] * lp[1:2], axis=-1, keepdims=True))
           - jnp.exp(jnp.sum(lp[2:3] * lp[3:4], axis=-1, keepdims=True)) + lam_init)

    def query_block(odd, qm, qm_next, out_rows):
        qi = 2 * step + odd
        m_scr[...] = jnp.full(m_scr.shape, -jnp.inf, F32)
        acc_scr[...] = jnp.zeros(acc_scr.shape, F32)

        def run(bounded, first, second):
            if not odd:
                @pl.when(step == 0)
                def _():
                    stage(bounded, [(0, qm, first)], None, None)

            def pair(i):
                stage(bounded, [(2 * i + 1, qm, second)], 2 * i, first)
                stage(bounded, [(2 * i + 2, qm, first)], 2 * i + 1, second)

            if bounded:
                def two_pairs(i, carry):
                    pair(2 * i)
                    pair(2 * i + 1)
                    return carry

                lax.fori_loop(0, step // 2, two_pairs, 0)

                @pl.when(step % 2 == 1)
                def _():
                    pair(step - 1)
            else:
                def one_pair(i, carry):
                    pair(i)
                    return carry

                lax.fori_loop(0, step, one_pair, 0)
            if odd:
                stage(bounded, [(qi, qm, second)], qi - 1, first)
                stage(bounded, [(0, qm_next, first)], qi, second, diagonal=True)
            else:
                stage(bounded, [(0, qm_next, second)], qi, first, diagonal=True)

        arrives_in = (step + odd) % 2
        for bounded in (True, False):
            @pl.when(bound_is_safe if bounded else jnp.logical_not(bound_is_safe))
            def _():
                run(bounded, s_scr.at[arrives_in], s_scr.at[1 - arrives_in])

        o_ref[0, out_rows, :] = ot_scr[...].T.astype(o_ref.dtype)

    query_block(0, qma_scr, qmb_scr, slice(0, t))
    query_block(1, qmb_scr, qmn_scr, slice(t, 2 * t))


def _attn_call(layer, score_bounds, qt, k, vt, lam_params, sub_gain, lam_init):
    b, s, _ = k.shape
    t = ATTN_TILE
    w = GROUP_WIDTH
    return pl.pallas_call(
        functools.partial(_attn_kernel, layer=layer, lam_init=lam_init),
        grid=(b, s // (2 * t)),
        in_specs=[
            pl.BlockSpec(memory_space=pltpu.SMEM),
            pl.BlockSpec((1, w, t), lambda i, j: (i, 0, 2 * j)),
            pl.BlockSpec((1, w, t), lambda i, j: (i, 0, 2 * j + 1)),
            pl.BlockSpec((1, w, t), lambda i, j: (i, 0, jnp.minimum(2 * j + 2, s // t - 1))),
            pl.BlockSpec((1, s, w), lambda i, j: (i, 0, 0)),
            pl.BlockSpec((1, w, s), lambda i, j: (i, 0, 0)),
            pl.BlockSpec((1, 4, QK_DIM), lambda i, j: (layer, 0, 0)),
            pl.BlockSpec((1, V_DIM, t), lambda i, j: (layer, 0, 0)),
        ],
        out_specs=pl.BlockSpec((1, 2 * t, w), lambda i, j: (i, j, 0)),
        out_shape=jax.ShapeDtypeStruct((b, s, w), BF16),
        scratch_shapes=[
            pltpu.VMEM((s // t, ATTN_HEADS * VT_ROWS, t), BF16),
            pltpu.VMEM((N_MAPS, w, t), BF16),
            pltpu.VMEM((N_MAPS, w, t), BF16),
            pltpu.VMEM((N_MAPS, w, t), BF16),
            pltpu.VMEM((2, N_MAPS, t, t), F32),
            pltpu.VMEM((N_MAPS, 1, t), F32),
            pltpu.VMEM((N_MAPS, VT_ROWS, t), F32),
            pltpu.VMEM((w, t), F32),
        ],
        compiler_params=pltpu.CompilerParams(
            dimension_semantics=("arbitrary", "arbitrary"), vmem_limit_bytes=VMEM_LIMIT),
        name="diff_attn",
    )(score_bounds, qt, qt, qt, k, vt, lam_params, sub_gain)


def _outproj_mlp_kernel(x_ref, ya_ref, yo_ref, glu0_ref, glun_ref, mod_ref, g_ref,
                        wout_ref, w1_ref, w2_ref, cw_ref, cbias_ref, lng_ref, lnb_ref,
                        o_ref, x1_scr, hb_scr, a_scr, yb_scr, hw_scr, ph_scr, *, layer):
    tm = x_ref.shape[1]
    w = GROUP_WIDTH
    d_ff = w1_ref.shape[1]
    j = pl.program_id(1)
    n_tiles = pl.num_programs(1)
    first_step = jnp.logical_and(pl.program_id(0) == 0, j == 0)
    d = x_ref.shape[2]
    batch_row = pl.ds(pl.program_id(0), 1)
    mod = lambda r: mod_ref[0, batch_row, r * d:(r + 1) * d]
    this_layer = pl.ds(layer, 1)

    def conformer_pieces():
        return _conformer_pieces(hw_scr, ph_scr, cw_ref.at[0], cbias_ref.at[this_layer],
                                 lng_ref.at[this_layer], lnb_ref.at[this_layer], yb_scr)

    @pl.when(first_step)
    def _():
        hw_scr[0:HALO, :] = jnp.zeros((HALO, w), F32)
        hw_scr[HALO:, :] = glu0_ref[0]
        for piece in conformer_pieces():
            piece()
        hw_scr[0:HALO, :] = hw_scr[tm:, :]

    mix = jnp.dot(ya_ref[0], wout_ref[0:w, :], preferred_element_type=F32)
    mix = mix + jnp.dot(yb_scr[...], wout_ref[w:2 * w, :], preferred_element_type=F32)
    mix = mix + jnp.dot(yo_ref[0], wout_ref[2 * w:, :], preferred_element_type=F32)
    x1 = x_ref[0] + mod(2) * mix
    x1_scr[...] = x1
    hb_scr[...] = _modulated_norm(x1, g_ref[this_layer, :], mod(3), mod(4)).astype(BF16)

    tail = hw_scr[0:HALO, :]
    hw_scr[0:HALO, :] = jnp.where(j == n_tiles - 1, 0.0, tail)
    hw_scr[HALO:, :] = glun_ref[0]
    pieces = conformer_pieces()
    pieces.pop(0)()

    always = j < n_tiles

    def anchor(ref, cols, slab):
        kept = ref[0:ANCHOR_ROWS, cols]
        ref[0:ANCHOR_ROWS, cols] = jnp.where(always, kept, slab.astype(kept.dtype))

    chunk = 1024
    n_chunks = d_ff // chunk
    per_chunk = len(pieces) // (2 * n_chunks)
    first_lanes = slice(0, LANES)

    def mixer_group(released_by, next_input, next_cols):
        bias = jnp.where(always, cbias_ref[this_layer, :], released_by[0:1, 0:w])
        for _ in range(per_chunk):
            anchor(next_input, next_cols, pieces.pop(0)(bias))

    released_by = x1
    for ci in range(n_chunks):
        c = ci * chunk
        a = jnp.dot(hb_scr[...], w1_ref[:, c:c + chunk], preferred_element_type=F32)
        a_scr[:, c:c + chunk] = jnp.square(jnp.maximum(a, 0.0)).astype(BF16)
        if ci + 1 < n_chunks:
            mixer_group(released_by, hb_scr, first_lanes)
        else:
            mixer_group(released_by, a_scr, first_lanes)
        released_by = a
    y = None
    for ci in range(n_chunks):
        c = ci * chunk
        part = jnp.dot(a_scr[:, c:c + chunk], w2_ref[c:c + chunk, :], preferred_element_type=F32)
        y = part if y is None else y + part
        if ci + 1 < n_chunks:
            mixer_group(released_by, a_scr, slice(c + chunk, c + chunk + LANES))
        else:
            mixer_group(released_by, x1_scr, first_lanes)
        released_by = part
    for piece in pieces:
        piece()
    hw_scr[0:HALO, :] = hw_scr[tm:, :]
    o_ref[0] = x1_scr[...] + mod(5) * y


def _outproj_mlp_call(layer, x, ya, glu, yo, mods, norm_g, w_out, w1, w2, conf_w, conf_b, ln_g, ln_b):
    b, s, d = x.shape
    d_ff = w1.shape[1]
    tm = ROW_TILE
    w = GROUP_WIDTH
    n_tiles = s // tm
    resident = lambda arr: pl.BlockSpec(arr.shape, lambda i, j: (0, 0), pipeline_mode=pl.Buffered(1))
    per_layer = lambda *shape: pl.BlockSpec((1,) + shape, lambda i, j: (layer,) + (0,) * len(shape))
    rows_of = lambda width: pl.BlockSpec((1, tm, width), lambda i, j: (i, j, 0))

    def next_tile(i, j):
        t = jnp.minimum(i * n_tiles + j + 1, b * n_tiles - 1)
        return (t // n_tiles, t % n_tiles, 0)

    whole = lambda arr: pl.BlockSpec(arr.shape, lambda i, j: (0,) * arr.ndim)
    return pl.pallas_call(
        functools.partial(_outproj_mlp_kernel, layer=layer),
        grid=(b, n_tiles),
        in_specs=[
            rows_of(d), rows_of(ya.shape[2]), rows_of(yo.shape[2]),
            pl.BlockSpec((1, tm, w), lambda i, j: (0, 0, 0)),
            pl.BlockSpec((1, tm, w), next_tile),
            per_layer(b, N_MOD * d),
            whole(norm_g),
            resident(w_out), resident(w1), resident(w2),
            per_layer(CONF_KERNEL, w), whole(conf_b), whole(ln_g), whole(ln_b),
        ],
        out_specs=rows_of(d),
        out_shape=jax.ShapeDtypeStruct((b, s, d), F32),
        scratch_shapes=[
            pltpu.VMEM((tm, d), F32),
            pltpu.VMEM((tm, d), BF16),
            pltpu.VMEM((tm, d_ff), BF16),
            pltpu.VMEM((tm, w), BF16),
            pltpu.VMEM((HALO + tm, w), F32),
            pltpu.VMEM((SUBLANES, HALO + tm, w), F32),
        ],
        compiler_params=pltpu.CompilerParams(
            dimension_semantics=("arbitrary", "arbitrary"), vmem_limit_bytes=VMEM_LIMIT),
        name="outproj_mlp",
    )(x, ya, yo, glu, glu, mods, norm_g, w_out, w1, w2, conf_w, conf_b, ln_g, ln_b)


def _block_diag_ones(width, group):
    idx = jnp.arange(width) // group
    return (idx[:, None] == idx[None, :]).astype(BF16)


def kernel(x, c, w_ada, b_ada, norm1_g, norm2_g, w_in, w_out, q_norm_g, k_norm_g, lam_params,
           attn_sub_g, conf_dw_w, conf_dw_b, conf_ln_g, conf_ln_b, short_conv_w, pool_w, pool_scale,
           w_ff1, w_ff2):
    depth, d, _ = w_in.shape
    b = x.shape[0]
    w = GROUP_WIDTH
    mods = _ada_call(c, w_ada, b_ada)
    ones32 = _block_diag_ones(w, QK_DIM)

    qk_gain = jnp.concatenate([jnp.tile(q_norm_g, (1, N_MAPS)) * (QK_DIM ** -0.5 * LOG2_E),
                               jnp.tile(k_norm_g, (1, N_MAPS))], axis=1)
    sub_gain = jnp.broadcast_to(attn_sub_g[:, :, None], (depth, V_DIM, ATTN_TILE))
    score_bounds = (1.02 * QK_DIM * (QK_DIM ** -0.5 * LOG2_E)
                    * jnp.max(jnp.abs(q_norm_g), axis=1) * jnp.max(jnp.abs(k_norm_g), axis=1))
    n_pool = len(POOL_WINDOWS)
    group_eye = jnp.eye(n_pool, dtype=pool_w.dtype)
    pool_bd = jnp.einsum('lgcd,gh->lgchd', pool_w, group_eye).reshape(depth, w, w).astype(BF16)
    w_in_b = w_in[0].astype(BF16)

    for l in range(depth):
        lam_init = 0.8 - 0.6 * math.exp(-0.3 * l)
        later = [(w_out, l), (w_ff1, l), (w_ff2, l)] + ([(w_in, l + 1)] if l + 1 < depth else [])
        qt, k, vt, glu, yo, w_out_b, w_ff1_b, w_ff2_b, *w_in_next = _inproj_mixers_call(
            l, x, mods, norm1_g, w_in_b, qk_gain, ones32,
            short_conv_w, pool_bd, pool_scale, later)
        if w_in_next:
            w_in_b = w_in_next[0]
        ya = _attn_call(l, score_bounds, qt, k, vt, lam_params, sub_gain, lam_init)
        x = _outproj_mlp_call(l, x, ya, glu, yo, mods, norm2_g, w_out_b, w_ff1_b, w_ff2_b,
                              conf_dw_w, conf_dw_b, conf_ln_g, conf_ln_b)
    return x
```
